```python
import jax, jax.numpy as jnp
from jax import lax
import numpy as np

D_MODEL = 1024
BATCH = 8
SEQ = 8192
DEPTH = 2

N_A_LAYERS = DEPTH // 2
N_B_LAYERS = DEPTH - N_A_LAYERS
RET_HEADS = 4
RET_QK_DIM = D_MODEL // RET_HEADS
RET_V_DIM = 2 * RET_QK_DIM
RET_QK_WIDTH = RET_HEADS * RET_QK_DIM
RET_V_WIDTH = RET_HEADS * RET_V_DIM
RET_IN_WIDTH = 2 * RET_QK_WIDTH + 2 * RET_V_WIDTH
RET_CHUNK = 128
ROPE_BASE = 10000.0
SB_HEADS = 16
SB_HEAD_DIM = D_MODEL // SB_HEADS
SB_BLOCK = 128
D_FF = -(-(8 * D_MODEL) // (3 * 256)) * 256
RMS_EPS = 1e-6
GN_EPS = 1e-5

kernel_name = "yoco_retention_stickbreaking_trunk"


def rmsnorm(x, g):
    xf = x.astype(jnp.float32)
    y = xf * lax.rsqrt(jnp.mean(xf * xf, axis=-1, keepdims=True) + RMS_EPS)
    return (y * g.astype(jnp.float32)).astype(x.dtype)


def rotary(t, pos):
    half = t.shape[-1] // 2
    inv_freq = ROPE_BASE ** (-jnp.arange(half, dtype=jnp.float32) / half)
    ang = pos[:, None] * inv_freq[None, :]
    cos, sin = jnp.cos(ang), jnp.sin(ang)
    t1, t2 = t[..., :half], t[..., half:]
    return jnp.concatenate([t1 * cos - t2 * sin, t1 * sin + t2 * cos], axis=-1)


def retention(h, w_in, gn_g, w_out):
    B, S, _ = h.shape
    H, dk, dv, C = RET_HEADS, RET_QK_DIM, RET_V_DIM, RET_CHUNK
    n = S // C
    proj = h @ w_in
    q, k, v, g = jnp.split(proj, [RET_QK_WIDTH, 2 * RET_QK_WIDTH, 2 * RET_QK_WIDTH + RET_V_WIDTH], axis=-1)
    q = q.reshape(B, S, H, dk).transpose(0, 2, 1, 3).astype(jnp.float32)
    k = k.reshape(B, S, H, dk).transpose(0, 2, 1, 3).astype(jnp.float32)
    v = v.reshape(B, S, H, dv).transpose(0, 2, 1, 3).astype(jnp.float32)
    pos = jnp.arange(S, dtype=jnp.float32)
    q = rotary(q, pos)
    k = rotary(k, pos) * (dk ** -0.5)

    log_g = jnp.log(1.0 - 2.0 ** (-5.0 - jnp.arange(H, dtype=jnp.float32)))
    j = jnp.arange(C, dtype=jnp.float32)
    rel = j[:, None] - j[None, :]
    inner_decay = jnp.where(rel[None] >= 0, jnp.exp(rel[None] * log_g[:, None, None]), 0.0)
    q_decay = jnp.exp((j[None, :] + 1.0) * log_g[:, None])
    k_decay = jnp.exp((C - 1.0 - j[None, :]) * log_g[:, None])
    chunk_decay = jnp.exp(C * log_g)

    def to_chunks(t):
        return t.reshape(B, H, n, C, t.shape[-1]).transpose(2, 0, 1, 3, 4)

    def step(state, qkv):
        qc, kc, vc = qkv
        scores = jnp.einsum('bhid,bhjd->bhij', qc, kc) * inner_decay[None]
        inner = jnp.einsum('bhij,bhjv->bhiv', scores, vc)
        cross = jnp.einsum('bhid,bhdv->bhiv', qc, state) * q_decay[None, :, :, None]
        state = state * chunk_decay[None, :, None, None] + jnp.einsum(
            'bhjd,bhjv->bhdv', kc * k_decay[None, :, :, None], vc)
        return state, inner + cross

    state0 = jnp.zeros((B, H, dk, dv), jnp.float32)
    _, o = lax.scan(step, state0, (to_chunks(q), to_chunks(k), to_chunks(v)))
    o = o.transpose(1, 2, 0, 3, 4).reshape(B, H, S, dv)
    mu = jnp.mean(o, axis=-1, keepdims=True)
    oc = o - mu
    o = oc * lax.rsqrt(jnp.mean(oc * oc, axis=-1, keepdims=True) + GN_EPS)
    o = o.transpose(0, 2, 1, 3).reshape(B, S, RET_V_WIDTH) * gn_g.astype(jnp.float32)
    y = (jax.nn.silu(g.astype(jnp.float32)) * o).astype(h.dtype)
    return y @ w_out


def shared_kv(x, kv_norm_g, w_kv):
    B, S, _ = x.shape
    kv = rmsnorm(x, kv_norm_g) @ w_kv
    k, v = jnp.split(kv, 2, axis=-1)
    k = k.reshape(B, S, SB_HEADS, SB_HEAD_DIM).transpose(0, 2, 1, 3)
    v = v.reshape(B, S, SB_HEADS, SB_HEAD_DIM).transpose(0, 2, 1, 3)
    return k, v


def stick_breaking(h, w_q, k, v, w_out):
    B, S, _ = h.shape
    H, dh, Bk = SB_HEADS, SB_HEAD_DIM, SB_BLOCK
    nb = S // Bk
    q = (h @ w_q).reshape(B, S, H, dh).transpose(0, 2, 1, 3) * (dh ** -0.5)
    q_blocks = q.reshape(B, H, nb, Bk, dh).transpose(2, 0, 1, 3, 4)
    idx = jnp.arange(Bk)
    diag_mask = idx[None, :] < idx[:, None]

    def one_query_block(args):
        qi, blk = args
        qi = qi.astype(jnp.float32)

        def body(step, carry):
            acc, logsurv = carry
            kb = blk - step
            kblk = lax.dynamic_slice_in_dim(k, kb * Bk, Bk, axis=2).astype(jnp.float32)
            vblk = lax.dynamic_slice_in_dim(v, kb * Bk, Bk, axis=2).astype(jnp.float32)
            z = jnp.einsum('bhid,bhjd->bhij', qi, kblk)
            valid = jnp.logical_or(diag_mask, step > 0)[None, None]
            neg_log_1m_beta = jnp.where(valid, jax.nn.softplus(z), 0.0)
            suffix = lax.cumsum(neg_log_1m_beta, axis=3, reverse=True) - neg_log_1m_beta
            log_a = -jax.nn.softplus(-z) - suffix - logsurv[..., None]
            a = jnp.where(valid, jnp.exp(log_a), 0.0)
            acc = acc + jnp.einsum('bhij,bhjd->bhid', a, vblk)
            logsurv = logsurv + jnp.sum(neg_log_1m_beta, axis=-1)
            return acc, logsurv

        init = (jnp.zeros((B, H, Bk, dh), jnp.float32), jnp.zeros((B, H, Bk), jnp.float32))
        acc, _ = lax.fori_loop(0, blk + 1, body, init)
        return acc

    o = lax.map(one_query_block, (q_blocks, jnp.arange(nb)))
    o = o.transpose(1, 0, 3, 2, 4).reshape(B, S, H * dh).astype(h.dtype)
    return o @ w_out


def swiglu(h, w_in, w_out):
    gate, up = jnp.split(h @ w_in, 2, axis=-1)
    return (jax.nn.silu(gate) * up) @ w_out


def setup_inputs(seed: int = 0) -> dict:
    key = jax.random.key(seed)
    ks = jax.random.split(key, 16)
    f32 = jnp.float32
    out_scale = (2.0 * DEPTH) ** -0.5

    def nrm(k, shape, fan_in, scale=1.0):
        return jax.random.normal(k, shape, f32) * (fan_in ** -0.5) * scale

    def gain(k, shape):
        return 1.0 + 0.02 * jax.random.normal(k, shape, f32)

    return {
        "x": jax.random.normal(ks[0], (BATCH, SEQ, D_MODEL), f32),
        "mix_norm_g": gain(ks[1], (DEPTH, D_MODEL)),
        "w_ret_in": nrm(ks[2], (N_A_LAYERS, D_MODEL, RET_IN_WIDTH), D_MODEL),
        "ret_gn_g": gain(ks[3], (N_A_LAYERS, RET_V_WIDTH)),
        "w_ret_out": nrm(ks[4], (N_A_LAYERS, RET_V_WIDTH, D_MODEL), RET_V_WIDTH, out_scale),
        "kv_norm_g": gain(ks[5], (D_MODEL,)),
        "w_kv": nrm(ks[6], (D_MODEL, 2 * D_MODEL), D_MODEL),
        "w_sb_q": nrm(ks[7], (N_B_LAYERS, D_MODEL, SB_HEADS * SB_HEAD_DIM), D_MODEL),
        "w_sb_out": nrm(ks[8], (N_B_LAYERS, SB_HEADS * SB_HEAD_DIM, D_MODEL), SB_HEADS * SB_HEAD_DIM, out_scale),
        "ffn_norm_g": gain(ks[9], (DEPTH, D_MODEL)),
        "w_ffn_in": nrm(ks[10], (DEPTH, D_MODEL, 2 * D_FF), D_MODEL),
        "w_ffn_out": nrm(ks[11], (DEPTH, D_FF, D_MODEL), D_FF, out_scale),
        "final_norm_g": gain(ks[12], (D_MODEL,)),
    }


def reference(x, mix_norm_g, w_ret_in, ret_gn_g, w_ret_out, kv_norm_g, w_kv,
              w_sb_q, w_sb_out, ffn_norm_g, w_ffn_in, w_ffn_out, final_norm_g):
    k_shared = None
    v_shared = None
    for layer in range(DEPTH):
        if layer < N_A_LAYERS:
            i = layer
            x = x + retention(rmsnorm(x, mix_norm_g[layer]), w_ret_in[i], ret_gn_g[i], w_ret_out[i])
        else:
            if layer == N_A_LAYERS:
                k_shared, v_shared = shared_kv(x, kv_norm_g, w_kv)
            i = layer - N_A_LAYERS
            x = x + stick_breaking(rmsnorm(x, mix_norm_g[layer]), w_sb_q[i], k_shared, v_shared, w_sb_out[i])
        x = x + swiglu(rmsnorm(x, ffn_norm_g[layer]), w_ffn_in[layer], w_ffn_out[layer])
    return rmsnorm(x, final_norm_g)
```

```python
import functools

import jax
import jax.numpy as jnp
from jax import lax
from jax.experimental import pallas as pl
from jax.experimental.pallas import tpu as pltpu

RMS_EPS = 1e-6
GN_EPS = 1e-5
RET_HEADS = 4
RET_CHUNK = 128
ROPE_BASE = 10000.0
SB_HEADS = 16
SB_BLOCK = 128

F32 = jnp.float32
BF16 = jnp.bfloat16

V7X_VMEM_BYTES = 64 * 1024 * 1024
VMEM_LIMIT_BYTES = (V7X_VMEM_BYTES * 7) // 8

NT_DIMS = (((1,), (1,)), ((), ()))
TN_DIMS = (((0,), (0,)), ((), ()))


def _params(*semantics):
    return pltpu.CompilerParams(dimension_semantics=semantics, vmem_limit_bytes=VMEM_LIMIT_BYTES)


def _rms_scale(x):
    return x * lax.rsqrt(jnp.mean(x * x, axis=-1, keepdims=True) + RMS_EPS)


def _silu(g):
    return g * (1.0 / (1.0 + jnp.exp(-g)))


def _dot(a, b):
    return jnp.dot(a, b, preferred_element_type=F32)


def _ret_proj_kernel(x_ref, g_ref, w_ref, cos_ref, sin_ref, q_ref, k_ref, v_ref, gate_ref, *, dk, k_scale):
    hb = (_rms_scale(x_ref[...]) * g_ref[...]).astype(BF16)
    cos = cos_ref[...]
    sin = sin_ref[...]
    half = dk // 2
    qk_width = q_ref.shape[1]
    v_width = v_ref.shape[1]
    for base, out_ref, scale in ((0, q_ref, 1.0), (qk_width, k_ref, k_scale)):
        for c in range(0, qk_width, dk):
            t = _dot(hb, w_ref[:, base + c:base + c + dk])
            t1, t2 = t[:, :half], t[:, half:]
            out_ref[:, c:c + half] = ((t1 * cos - t2 * sin) * scale).astype(BF16)
            out_ref[:, c + half:c + dk] = ((t1 * sin + t2 * cos) * scale).astype(BF16)
    for base, out_ref in ((2 * qk_width, v_ref), (2 * qk_width + v_width, gate_ref)):
        for c in range(0, v_width, 2 * dk):
            out_ref[:, c:c + 2 * dk] = _dot(hb, w_ref[:, base + c:base + c + 2 * dk]).astype(BF16)


def _ret_proj(x2, g, w, cos, sin, *, seq, tm=256):
    T, D = x2.shape
    dk = D // RET_HEADS
    qk_width, v_width = D, 2 * D
    sblocks = seq // tm
    row = lambda i: (i, 0)
    return pl.pallas_call(
        functools.partial(_ret_proj_kernel, dk=dk, k_scale=dk ** -0.5),
        grid=(T // tm,),
        in_specs=[
            pl.BlockSpec((tm, D), row),
            pl.BlockSpec((1, D), lambda i: (0, 0)),
            pl.BlockSpec(w.shape, lambda i: (0, 0)),
            pl.BlockSpec((tm, dk // 2), lambda i: (i % sblocks, 0)),
            pl.BlockSpec((tm, dk // 2), lambda i: (i % sblocks, 0)),
        ],
        out_specs=[
            pl.BlockSpec((tm, qk_width), row),
            pl.BlockSpec((tm, qk_width), row),
            pl.BlockSpec((tm, v_width), row),
            pl.BlockSpec((tm, v_width), row),
        ],
        out_shape=[
            jax.ShapeDtypeStruct((T, qk_width), BF16),
            jax.ShapeDtypeStruct((T, qk_width), BF16),
            jax.ShapeDtypeStruct((T, v_width), BF16),
            jax.ShapeDtypeStruct((T, v_width), BF16),
        ],
        compiler_params=_params("parallel"),
        name="ret_proj",
    )(x2, g, w, cos, sin)


def _retention_kernel(q_ref, k_ref, v_ref, gate_ref, gn_ref, dmat_ref, qdec_ref, kdec_ref, cdec_ref,
                      y_ref, state_ref, *, chunk):
    @pl.when(pl.program_id(2) == 0)
    def _():
        state_ref[...] = jnp.zeros_like(state_ref)

    dmat = dmat_ref[0]
    qdec = qdec_ref[0]
    kdec = kdec_ref[0]
    cdec = cdec_ref[0]
    gn = gn_ref[...]
    for c in range(q_ref.shape[0] // chunk):
        rows = slice(c * chunk, (c + 1) * chunk)
        qc = q_ref[rows, :]
        kc = k_ref[rows, :]
        vc = v_ref[rows, :]
        state = state_ref[...]
        scores = lax.dot_general(qc, kc, NT_DIMS, preferred_element_type=F32) * dmat
        o = _dot(scores.astype(BF16), vc) + _dot(qc, state.astype(BF16)) * qdec
        kd = (kc.astype(F32) * kdec).astype(BF16)
        state_ref[...] = state * cdec + lax.dot_general(kd, vc, TN_DIMS, preferred_element_type=F32)
        oc = o - jnp.mean(o, axis=-1, keepdims=True)
        on = oc * lax.rsqrt(jnp.mean(oc * oc, axis=-1, keepdims=True) + GN_EPS) * gn
        y_ref[rows, :] = (_silu(gate_ref[rows, :].astype(F32)) * on).astype(BF16)


def _retention(q, k, v, gate, gn_g, *, batch, seq, rows_per_step=512):
    T = q.shape[0]
    H, C = RET_HEADS, RET_CHUNK
    dk = q.shape[1] // H
    dv = v.shape[1] // H
    nsb = seq // rows_per_step

    log_g = jnp.log(1.0 - 2.0 ** (-5.0 - jnp.arange(H, dtype=F32)))
    j = jnp.arange(C, dtype=F32)
    rel = j[:, None] - j[None, :]
    dmat = jnp.where(rel[None] >= 0, jnp.exp(rel[None] * log_g[:, None, None]), 0.0)
    qdec = jnp.exp((j[None, :] + 1.0) * log_g[:, None])[:, :, None]
    kdec = jnp.exp((C - 1.0 - j[None, :]) * log_g[:, None])[:, :, None]
    cdec = jnp.exp(C * log_g)[:, None, None]

    tok = lambda b, h, s: (b * nsb + s, h)
    per_head = lambda b, h, s: (h, 0, 0)
    return pl.pallas_call(
        functools.partial(_retention_kernel, chunk=C),
        grid=(batch, H, nsb),
        in_specs=[
            pl.BlockSpec((rows_per_step, dk), tok),
            pl.BlockSpec((rows_per_step, dk), tok),
            pl.BlockSpec((rows_per_step, dv), tok),
            pl.BlockSpec((rows_per_step, dv), tok),
            pl.BlockSpec((1, dv), lambda b, h, s: (0, h)),
            pl.BlockSpec((1, C, C), per_head),
            pl.BlockSpec((1, C, 1), per_head),
            pl.BlockSpec((1, C, 1), per_head),
            pl.BlockSpec((1, 1, 1), per_head),
        ],
        out_specs=pl.BlockSpec((rows_per_step, dv), tok),
        out_shape=jax.ShapeDtypeStruct((T, v.shape[1]), BF16),
        scratch_shapes=[pltpu.VMEM((dk, dv), F32)],
        compiler_params=_params("parallel", "parallel", "arbitrary"),
        name="retention",
    )(q, k, v, gate, gn_g, dmat, qdec, kdec, cdec)


def _out_proj_kernel(x_ref, y_ref, w_ref, o_ref):
    o_ref[...] = x_ref[...] + _dot(y_ref[...], w_ref[...])


def _out_proj(x2, y, w, *, tm=512):
    T, D = x2.shape
    row = lambda i: (i, 0)
    return pl.pallas_call(
        _out_proj_kernel,
        grid=(T // tm,),
        in_specs=[
            pl.BlockSpec((tm, D), row),
            pl.BlockSpec((tm, y.shape[1]), row),
            pl.BlockSpec(w.shape, lambda i: (0, 0)),
        ],
        out_specs=pl.BlockSpec((tm, D), row),
        out_shape=jax.ShapeDtypeStruct((T, D), F32),
        compiler_params=_params("parallel"),
        name="out_proj",
    )(x2, y, w)


def _ffn_kernel(x_ref, g_ref, win_ref, wout_ref, fg_ref, o_ref, *, d_ff, chunk, final_norm):
    x = x_ref[...]
    hb = (_rms_scale(x) * g_ref[...]).astype(BF16)
    o_ref[...] = x
    for c in range(0, d_ff, chunk):
        gate = _dot(hb, win_ref[:, c:c + chunk])
        up = _dot(hb, win_ref[:, d_ff + c:d_ff + c + chunk])
        act = (_silu(gate) * up).astype(BF16)
        o_ref[...] += _dot(act, wout_ref[c:c + chunk, :])
    if final_norm:
        o_ref[...] = _rms_scale(o_ref[...]) * fg_ref[...]


def _ffn(x2, g, w_in, w_out, final_g, *, final_norm, tm=512, chunk=256):
    T, D = x2.shape
    d_ff = w_out.shape[0]
    row = lambda i: (i, 0)
    const = lambda i: (0, 0)
    return pl.pallas_call(
        functools.partial(_ffn_kernel, d_ff=d_ff, chunk=chunk, final_norm=final_norm),
        grid=(T // tm,),
        in_specs=[
            pl.BlockSpec((tm, D), row),
            pl.BlockSpec((1, D), const),
            pl.BlockSpec(w_in.shape, const),
            pl.BlockSpec(w_out.shape, const),
            pl.BlockSpec((1, D), const),
        ],
        out_specs=pl.BlockSpec((tm, D), row),
        out_shape=jax.ShapeDtypeStruct((T, D), F32),
        compiler_params=_params("parallel"),
        name="ffn_final" if final_norm else "ffn",
    )(x2, g, w_in, w_out, final_g)


def _sb_proj_kernel(x_ref, gkv_ref, gq_ref, wkv_ref, wq_ref, q_ref, k_ref, v_ref, *, dh, q_scale, group):
    r = _rms_scale(x_ref[...])
    hkv = (r * gkv_ref[...]).astype(BF16)
    hq = (r * gq_ref[...]).astype(BF16)
    D = wq_ref.shape[1]
    width = group * dh
    for c in range(0, D, width):
        tq = _dot(hq, wq_ref[:, c:c + width]) * q_scale
        tk = _dot(hkv, wkv_ref[:, c:c + width])
        tv = _dot(hkv, wkv_ref[:, D + c:D + c + width])
        for g in range(group):
            h = c // dh + g
            lanes = slice(g * dh, (g + 1) * dh)
            q_ref[0, h] = tq[:, lanes].astype(BF16)
            k_ref[0, h] = tk[:, lanes].astype(BF16)
            v_ref[0, h] = tv[:, lanes].astype(BF16)


def _sb_proj(x2, gkv, gq, wkv, wq, *, batch, seq, tm=512):
    T, D = x2.shape
    H = SB_HEADS
    dh = D // H
    sblocks = seq // tm
    const = lambda i: (0, 0)
    head_major = pl.BlockSpec((1, H, tm, dh), lambda i: (i // sblocks, 0, i % sblocks, 0))
    shape = jax.ShapeDtypeStruct((batch, H, seq, dh), BF16)
    return pl.pallas_call(
        functools.partial(_sb_proj_kernel, dh=dh, q_scale=dh ** -0.5, group=4),
        grid=(T // tm,),
        in_specs=[
            pl.BlockSpec((tm, D), lambda i: (i, 0)),
            pl.BlockSpec((1, D), const),
            pl.BlockSpec((1, D), const),
            pl.BlockSpec(wkv.shape, const),
            pl.BlockSpec(wq.shape, const),
        ],
        out_specs=[head_major, head_major, head_major],
        out_shape=[shape, shape, shape],
        compiler_params=_params("parallel"),
        name="sb_proj",
    )(x2, gkv, gq, wkv, wq)


def _softplus(z):
    return jnp.maximum(z, 0.0) + jnp.log(1.0 + jnp.exp(-jnp.abs(z)))


def _sb_attn_kernel(q_ref, k_ref, v_ref, tri_ref, o_ref, *, blk):
    seq = q_ref.shape[2]
    tri = tri_ref[...]
    row = lax.broadcasted_iota(jnp.int32, (blk, blk), 0)
    col = lax.broadcasted_iota(jnp.int32, (blk, blk), 1)
    causal = col < row

    def visit(q, kb, acc, logsurv, diagonal):
        keys = pl.ds(pl.multiple_of(kb * blk, blk), blk)
        z = lax.dot_general(q, k_ref[0, 0, keys, :], NT_DIMS, preferred_element_type=F32)
        sp = _softplus(z)
        if diagonal:
            sp = jnp.where(causal, sp, 0.0)
        hi = sp.astype(BF16)
        lo = (sp - hi.astype(F32)).astype(BF16)
        sums = _dot(hi, tri) + _dot(lo, tri)
        log_a = (z - sp) - sums[:, :blk] - logsurv
        a = jnp.exp(log_a)
        if diagonal:
            a = jnp.where(causal, a, 0.0)
        acc = acc + _dot(a.astype(BF16), v_ref[0, 0, keys, :])
        return acc, logsurv + sums[:, blk:]

    def query_block(qi, carry):
        rows = pl.ds(pl.multiple_of(qi * blk, blk), blk)
        q = q_ref[0, 0, rows, :]
        acc = jnp.zeros((blk, q.shape[1]), F32)
        logsurv = jnp.zeros((blk, blk), F32)
        acc, logsurv = visit(q, qi, acc, logsurv, True)

        def earlier(step, c):
            return visit(q, qi - 1 - step, c[0], c[1], False)

        acc, _ = lax.fori_loop(0, qi, earlier, (acc, logsurv))
        o_ref[0, 0, rows, :] = acc.astype(o_ref.dtype)
        return carry

    lax.fori_loop(0, seq // blk, query_block, 0)


def _sb_attn(q, k, v):
    B, H, S, dh = q.shape
    blk = SB_BLOCK
    idx = jnp.arange(blk)
    tri = jnp.concatenate(
        [(idx[:, None] > idx[None, :]).astype(BF16), jnp.ones((blk, blk), BF16)], axis=1)
    per_head = pl.BlockSpec((1, 1, S, dh), lambda b, h: (b, h, 0, 0))
    return pl.pallas_call(
        functools.partial(_sb_attn_kernel, blk=blk),
        grid=(B, H),
        in_specs=[per_head, per_head, per_head, pl.BlockSpec(tri.shape, lambda b, h: (0, 0))],
        out_specs=per_head,
        out_shape=jax.ShapeDtypeStruct((B, H, S, dh), BF16),
        compiler_params=_params("parallel", "parallel"),
        name="sb_attn",
    )(q, k, v, tri)


def _sb_out_kernel(x_ref, o_ref, w_ref, out_ref):
    out_ref[...] = x_ref[...]
    for h in range(o_ref.shape[1]):
        out_ref[...] += _dot(o_ref[0, h], w_ref[h])


def _sb_out(x2, o, w, *, seq, tm=512):
    T, D = x2.shape
    _, H, _, dh = o.shape
    sblocks = seq // tm
    row = lambda i: (i, 0)
    return pl.pallas_call(
        _sb_out_kernel,
        grid=(T // tm,),
        in_specs=[
            pl.BlockSpec((tm, D), row),
            pl.BlockSpec((1, H, tm, dh), lambda i: (i // sblocks, 0, i % sblocks, 0)),
            pl.BlockSpec((H, dh, D), lambda i: (0, 0, 0)),
        ],
        out_specs=pl.BlockSpec((tm, D), row),
        out_shape=jax.ShapeDtypeStruct((T, D), F32),
        compiler_params=_params("parallel"),
        name="sb_out",
    )(x2, o, w.reshape(H, dh, D))


def _rotary_tables(seq, dk):
    half = dk // 2
    inv_freq = ROPE_BASE ** (-jnp.arange(half, dtype=F32) / half)
    ang = jnp.arange(seq, dtype=F32)[:, None] * inv_freq[None, :]
    return jnp.cos(ang), jnp.sin(ang)


def kernel(x, mix_norm_g, w_ret_in, ret_gn_g, w_ret_out, kv_norm_g, w_kv, w_sb_q, w_sb_out,
           ffn_norm_g, w_ffn_in, w_ffn_out, final_norm_g):
    B, S, D = x.shape
    depth = mix_norm_g.shape[0]
    n_ret = w_ret_in.shape[0]
    assert w_sb_q.shape[0] == 1 and depth == n_ret + 1
    x2 = x.reshape(B * S, D)
    cos, sin = _rotary_tables(S, D // RET_HEADS)
    final_g = final_norm_g.reshape(1, D)

    for layer in range(depth):
        mix_g = mix_norm_g[layer].reshape(1, D)
        if layer < n_ret:
            q, k, v, gate = _ret_proj(x2, mix_g, w_ret_in[layer].astype(BF16), cos, sin, seq=S)
            y = _retention(q, k, v, gate, ret_gn_g[layer].reshape(1, -1), batch=B, seq=S)
            x2 = _out_proj(x2, y, w_ret_out[layer].astype(BF16))
        else:
            q, k_sb, v_sb = _sb_proj(x2, kv_norm_g.reshape(1, D), mix_g,
                                     w_kv.astype(BF16), w_sb_q[0].astype(BF16), batch=B, seq=S)
            o = _sb_attn(q, k_sb, v_sb)
            x2 = _sb_out(x2, o, w_sb_out[0].astype(BF16), seq=S)
        x2 = _ffn(x2, ffn_norm_g[layer].reshape(1, D), w_ffn_in[layer].astype(BF16),
                  w_ffn_out[layer].astype(BF16), final_g, final_norm=(layer == depth - 1))
    return x2.reshape(B, S, D)
```

```python
import functools

import jax
import jax.numpy as jnp
from jax import lax
from jax.experimental import pallas as pl
from jax.experimental.pallas import tpu as pltpu

RMS_EPS = 1e-6
GN_EPS = 1e-5
RET_HEADS = 4
RET_CHUNK = 128
ROPE_BASE = 10000.0
SB_HEADS = 16
SB_BLOCK = 128

LOG2_E = 1.4426950408889634

F32 = jnp.float32
BF16 = jnp.bfloat16

V7X_VMEM_BYTES = 64 * 1024 * 1024
VMEM_LIMIT_BYTES = (V7X_VMEM_BYTES * 7) // 8

NT_DIMS = (((1,), (1,)), ((), ()))
TN_DIMS = (((0,), (0,)), ((), ()))


def _params(*semantics):
    return pltpu.CompilerParams(dimension_semantics=semantics, vmem_limit_bytes=VMEM_LIMIT_BYTES)


def _rms_scale(x):
    return x * lax.rsqrt(jnp.mean(x * x, axis=-1, keepdims=True) + RMS_EPS)


def _silu(g):
    return g * (1.0 / (1.0 + jnp.exp(-g)))


def _dot(a, b):
    return jnp.dot(a, b, preferred_element_type=F32)


def _ret_proj_kernel(x_ref, g_ref, w_ref, cos_ref, sin_ref, q_ref, k_ref, v_ref, gate_ref, *, dk, k_scale):
    hb = (_rms_scale(x_ref[...]) * g_ref[...]).astype(BF16)
    cos = cos_ref[...]
    sin = sin_ref[...]
    half = dk // 2
    qk_width = q_ref.shape[1]
    v_width = v_ref.shape[1]
    for base, out_ref, scale in ((0, q_ref, 1.0), (qk_width, k_ref, k_scale)):
        for c in range(0, qk_width, dk):
            t = _dot(hb, w_ref[:, base + c:base + c + dk])
            t1, t2 = t[:, :half], t[:, half:]
            out_ref[:, c:c + half] = ((t1 * cos - t2 * sin) * scale).astype(BF16)
            out_ref[:, c + half:c + dk] = ((t1 * sin + t2 * cos) * scale).astype(BF16)
    for base, out_ref in ((2 * qk_width, v_ref), (2 * qk_width + v_width, gate_ref)):
        for c in range(0, v_width, 2 * dk):
            out_ref[:, c:c + 2 * dk] = _dot(hb, w_ref[:, base + c:base + c + 2 * dk]).astype(BF16)


def _ret_proj(x2, g, w, cos, sin, *, seq, tm=256):
    T, D = x2.shape
    dk = D // RET_HEADS
    qk_width, v_width = D, 2 * D
    sblocks = seq // tm
    row = lambda i: (i, 0)
    return pl.pallas_call(
        functools.partial(_ret_proj_kernel, dk=dk, k_scale=dk ** -0.5),
        grid=(T // tm,),
        in_specs=[
            pl.BlockSpec((tm, D), row),
            pl.BlockSpec((1, D), lambda i: (0, 0)),
            pl.BlockSpec(w.shape, lambda i: (0, 0)),
            pl.BlockSpec((tm, dk // 2), lambda i: (i % sblocks, 0)),
            pl.BlockSpec((tm, dk // 2), lambda i: (i % sblocks, 0)),
        ],
        out_specs=[
            pl.BlockSpec((tm, qk_width), row),
            pl.BlockSpec((tm, qk_width), row),
            pl.BlockSpec((tm, v_width), row),
            pl.BlockSpec((tm, v_width), row),
        ],
        out_shape=[
            jax.ShapeDtypeStruct((T, qk_width), BF16),
            jax.ShapeDtypeStruct((T, qk_width), BF16),
            jax.ShapeDtypeStruct((T, v_width), BF16),
            jax.ShapeDtypeStruct((T, v_width), BF16),
        ],
        compiler_params=_params("parallel"),
        name="ret_proj",
    )(x2, g, w, cos, sin)


def _retention_kernel(q_ref, k_ref, v_ref, gate_ref, gn_ref, dmat_ref, qdec_ref, kdec_ref, cdec_ref,
                      y_ref, state_ref, *, chunk):
    @pl.when(pl.program_id(2) == 0)
    def _():
        state_ref[...] = jnp.zeros_like(state_ref)

    dmat = dmat_ref[0]
    qdec = qdec_ref[0]
    kdec = kdec_ref[0]
    cdec = cdec_ref[0]
    gn = gn_ref[...]
    for c in range(q_ref.shape[0] // chunk):
        rows = slice(c * chunk, (c + 1) * chunk)
        qc = q_ref[rows, :]
        kc = k_ref[rows, :]
        vc = v_ref[rows, :]
        state = state_ref[...]
        scores = lax.dot_general(qc, kc, NT_DIMS, preferred_element_type=F32) * dmat
        o = _dot(scores.astype(BF16), vc) + _dot(qc, state.astype(BF16)) * qdec
        kd = (kc.astype(F32) * kdec).astype(BF16)
        state_ref[...] = state * cdec + lax.dot_general(kd, vc, TN_DIMS, preferred_element_type=F32)
        oc = o - jnp.mean(o, axis=-1, keepdims=True)
        on = oc * lax.rsqrt(jnp.mean(oc * oc, axis=-1, keepdims=True) + GN_EPS) * gn
        y_ref[rows, :] = (_silu(gate_ref[rows, :].astype(F32)) * on).astype(BF16)


def _retention(q, k, v, gate, gn_g, *, batch, seq, rows_per_step=512):
    T = q.shape[0]
    H, C = RET_HEADS, RET_CHUNK
    dk = q.shape[1] // H
    dv = v.shape[1] // H
    nsb = seq // rows_per_step

    log_g = jnp.log(1.0 - 2.0 ** (-5.0 - jnp.arange(H, dtype=F32)))
    j = jnp.arange(C, dtype=F32)
    rel = j[:, None] - j[None, :]
    dmat = jnp.where(rel[None] >= 0, jnp.exp(rel[None] * log_g[:, None, None]), 0.0)
    qdec = jnp.exp((j[None, :] + 1.0) * log_g[:, None])[:, :, None]
    kdec = jnp.exp((C - 1.0 - j[None, :]) * log_g[:, None])[:, :, None]
    cdec = jnp.exp(C * log_g)[:, None, None]

    tok = lambda b, h, s: (b * nsb + s, h)
    per_head = lambda b, h, s: (h, 0, 0)
    return pl.pallas_call(
        functools.partial(_retention_kernel, chunk=C),
        grid=(batch, H, nsb),
        in_specs=[
            pl.BlockSpec((rows_per_step, dk), tok),
            pl.BlockSpec((rows_per_step, dk), tok),
            pl.BlockSpec((rows_per_step, dv), tok),
            pl.BlockSpec((rows_per_step, dv), tok),
            pl.BlockSpec((1, dv), lambda b, h, s: (0, h)),
            pl.BlockSpec((1, C, C), per_head),
            pl.BlockSpec((1, C, 1), per_head),
            pl.BlockSpec((1, C, 1), per_head),
            pl.BlockSpec((1, 1, 1), per_head),
        ],
        out_specs=pl.BlockSpec((rows_per_step, dv), tok),
        out_shape=jax.ShapeDtypeStruct((T, v.shape[1]), BF16),
        scratch_shapes=[pltpu.VMEM((dk, dv), F32)],
        compiler_params=_params("parallel", "parallel", "arbitrary"),
        name="retention",
    )(q, k, v, gate, gn_g, dmat, qdec, kdec, cdec)


def _out_proj_kernel(x_ref, y_ref, w_ref, o_ref):
    o_ref[...] = x_ref[...] + _dot(y_ref[...], w_ref[...])


def _out_proj(x2, y, w, *, tm=512):
    T, D = x2.shape
    row = lambda i: (i, 0)
    return pl.pallas_call(
        _out_proj_kernel,
        grid=(T // tm,),
        in_specs=[
            pl.BlockSpec((tm, D), row),
            pl.BlockSpec((tm, y.shape[1]), row),
            pl.BlockSpec(w.shape, lambda i: (0, 0)),
        ],
        out_specs=pl.BlockSpec((tm, D), row),
        out_shape=jax.ShapeDtypeStruct((T, D), F32),
        compiler_params=_params("parallel"),
        name="out_proj",
    )(x2, y, w)


def _ffn_kernel(x_ref, g_ref, win_ref, wout_ref, fg_ref, o_ref, *, d_ff, chunk, final_norm):
    x = x_ref[...]
    hb = (_rms_scale(x) * g_ref[...]).astype(BF16)
    o_ref[...] = x
    for c in range(0, d_ff, chunk):
        gate = _dot(hb, win_ref[:, c:c + chunk])
        up = _dot(hb, win_ref[:, d_ff + c:d_ff + c + chunk])
        act = (_silu(gate) * up).astype(BF16)
        o_ref[...] += _dot(act, wout_ref[c:c + chunk, :])
    if final_norm:
        o_ref[...] = _rms_scale(o_ref[...]) * fg_ref[...]


def _ffn(x2, g, w_in, w_out, final_g, *, final_norm, tm=512, chunk=256):
    T, D = x2.shape
    d_ff = w_out.shape[0]
    row = lambda i: (i, 0)
    const = lambda i: (0, 0)
    return pl.pallas_call(
        functools.partial(_ffn_kernel, d_ff=d_ff, chunk=chunk, final_norm=final_norm),
        grid=(T // tm,),
        in_specs=[
            pl.BlockSpec((tm, D), row),
            pl.BlockSpec((1, D), const),
            pl.BlockSpec(w_in.shape, const),
            pl.BlockSpec(w_out.shape, const),
            pl.BlockSpec((1, D), const),
        ],
        out_specs=pl.BlockSpec((tm, D), row),
        out_shape=jax.ShapeDtypeStruct((T, D), F32),
        compiler_params=_params("parallel"),
        name="ffn_final" if final_norm else "ffn",
    )(x2, g, w_in, w_out, final_g)


def _sb_proj_kernel(x_ref, gkv_ref, gq_ref, wkv_ref, wq_ref, q_ref, k_ref, v_ref, *, dh, q_scale, group):
    r = _rms_scale(x_ref[...])
    hkv = (r * gkv_ref[...]).astype(BF16)
    hq = (r * gq_ref[...]).astype(BF16)
    D = wq_ref.shape[1]
    width = group * dh
    for c in range(0, D, width):
        tq = _dot(hq, wq_ref[:, c:c + width]) * q_scale
        tk = _dot(hkv, wkv_ref[:, c:c + width])
        tv = _dot(hkv, wkv_ref[:, D + c:D + c + width])
        for g in range(group):
            h = c // dh + g
            lanes = slice(g * dh, (g + 1) * dh)
            q_ref[0, h] = tq[:, lanes].astype(BF16)
            k_ref[0, h] = tk[:, lanes].astype(BF16)
            v_ref[0, h] = tv[:, lanes].astype(BF16)


def _sb_proj(x2, gkv, gq, wkv, wq, *, batch, seq, tm=512):
    T, D = x2.shape
    H = SB_HEADS
    dh = D // H
    sblocks = seq // tm
    const = lambda i: (0, 0)
    head_major = pl.BlockSpec((1, H, tm, dh), lambda i: (i // sblocks, 0, i % sblocks, 0))
    shape = jax.ShapeDtypeStruct((batch, H, seq, dh), BF16)
    return pl.pallas_call(
        functools.partial(_sb_proj_kernel, dh=dh, q_scale=dh ** -0.5 * LOG2_E, group=4),
        grid=(T // tm,),
        in_specs=[
            pl.BlockSpec((tm, D), lambda i: (i, 0)),
            pl.BlockSpec((1, D), const),
            pl.BlockSpec((1, D), const),
            pl.BlockSpec(wkv.shape, const),
            pl.BlockSpec(wq.shape, const),
        ],
        out_specs=[head_major, head_major, head_major],
        out_shape=[shape, shape, shape],
        compiler_params=_params("parallel"),
        name="sb_proj",
    )(x2, gkv, gq, wkv, wq)


MASKED_LOG2 = -1e30


def _sb_attn_kernel(q_ref, k_ref, v_ref, tri_ref, o_ref, m_ref, tot_ref, acc_ref, ls_ref, *, tq, tk):
    seq = q_ref.shape[2]
    band = tq // tk
    assert band == 2
    sign_bit = jnp.uint32(0x80000000)

    def front(slot, q0, key0, diag_shift):
        z = lax.dot_general(q_ref[0, 0, pl.ds(q0, tq), :], k_ref[0, 0, pl.ds(key0, tk), :], NT_DIMS,
                            preferred_element_type=F32)
        neg_abs = pltpu.bitcast(pltpu.bitcast(z, jnp.uint32) | sign_bit, F32)
        sp = jnp.maximum(z, 0.0) + jnp.log2(1.0 + jnp.exp2(neg_abs))
        if diag_shift is not None:
            valid = (lax.broadcasted_iota(jnp.int32, (tq, tk), 1) + diag_shift
                     < lax.broadcasted_iota(jnp.int32, (tq, tk), 0))
            sp = jnp.where(valid, sp, 0.0)
        m = (z - sp) - _dot(sp.astype(BF16), tri_ref[...])
        if diag_shift is not None:
            m = jnp.where(valid, m, MASKED_LOG2)
        m_ref[slot] = m
        tot_ref[slot] = jnp.sum(sp, axis=-1, keepdims=True)

    def back(slot, key0):
        a = jnp.exp2(m_ref[slot] - ls_ref[...])
        acc_ref[...] += _dot(a.astype(BF16), v_ref[0, 0, pl.ds(key0, tk), :])
        ls_ref[...] += tot_ref[slot]

    def query_tile(qt, carry):
        q0 = pl.multiple_of(qt * tq, tq)
        acc_ref[...] = jnp.zeros_like(acc_ref)
        ls_ref[...] = jnp.zeros_like(ls_ref)
        front(0, q0, q0 + tk, tk)
        front(1, q0, q0, 0)
        back(0, q0 + tk)

        def two_tiles(i, c):
            key_a = pl.multiple_of(q0 - (2 * i + 1) * tk, tk)
            key_b = pl.multiple_of(key_a - tk, tk)
            front(0, q0, key_a, None)
            back(1, key_a + tk)
            front(1, q0, key_b, None)
            back(0, key_a)
            return c

        lax.fori_loop(0, qt, two_tiles, 0)
        back(1, 0)
        o_ref[0, 0, pl.ds(q0, tq), :] = acc_ref[...].astype(o_ref.dtype)
        return carry

    lax.fori_loop(0, seq // tq, query_tile, 0)


def _sb_attn(q, k, v, *, tq=512, tk=256):
    B, H, S, dh = q.shape
    idx = jnp.arange(tk)
    tri = (idx[:, None] > idx[None, :]).astype(BF16)
    per_head = pl.BlockSpec((1, 1, S, dh), lambda b, h: (b, h, 0, 0))
    return pl.pallas_call(
        functools.partial(_sb_attn_kernel, tq=tq, tk=tk),
        grid=(B, H),
        in_specs=[per_head, per_head, per_head, pl.BlockSpec(tri.shape, lambda b, h: (0, 0))],
        out_specs=per_head,
        out_shape=jax.ShapeDtypeStruct((B, H, S, dh), BF16),
        scratch_shapes=[pltpu.VMEM((2, tq, tk), F32), pltpu.VMEM((2, tq, 1), F32),
                        pltpu.VMEM((tq, dh), F32), pltpu.VMEM((tq, 1), F32)],
        compiler_params=_params("parallel", "parallel"),
        name="sb_attn",
    )(q, k, v, tri)


def _sb_out_kernel(x_ref, o_ref, w_ref, out_ref):
    out_ref[...] = x_ref[...]
    for h in range(o_ref.shape[1]):
        out_ref[...] += _dot(o_ref[0, h], w_ref[h])


def _sb_out(x2, o, w, *, seq, tm=512):
    T, D = x2.shape
    _, H, _, dh = o.shape
    sblocks = seq // tm
    row = lambda i: (i, 0)
    return pl.pallas_call(
        _sb_out_kernel,
        grid=(T // tm,),
        in_specs=[
            pl.BlockSpec((tm, D), row),
            pl.BlockSpec((1, H, tm, dh), lambda i: (i // sblocks, 0, i % sblocks, 0)),
            pl.BlockSpec((H, dh, D), lambda i: (0, 0, 0)),
        ],
        out_specs=pl.BlockSpec((tm, D), row),
        out_shape=jax.ShapeDtypeStruct((T, D), F32),
        compiler_params=_params("parallel"),
        name="sb_out",
    )(x2, o, w.reshape(H, dh, D))


def _rotary_tables(seq, dk):
    half = dk // 2
    inv_freq = ROPE_BASE ** (-jnp.arange(half, dtype=F32) / half)
    ang = jnp.arange(seq, dtype=F32)[:, None] * inv_freq[None, :]
    return jnp.cos(ang), jnp.sin(ang)


def kernel(x, mix_norm_g, w_ret_in, ret_gn_g, w_ret_out, kv_norm_g, w_kv, w_sb_q, w_sb_out,
           ffn_norm_g, w_ffn_in, w_ffn_out, final_norm_g):
    B, S, D = x.shape
    depth = mix_norm_g.shape[0]
    n_ret = w_ret_in.shape[0]
    assert w_sb_q.shape[0] == 1 and depth == n_ret + 1
    x2 = x.reshape(B * S, D)
    cos, sin = _rotary_tables(S, D // RET_HEADS)
    final_g = final_norm_g.reshape(1, D)

    for layer in range(depth):
        mix_g = mix_norm_g[layer].reshape(1, D)
        if layer < n_ret:
            q, k, v, gate = _ret_proj(x2, mix_g, w_ret_in[layer].astype(BF16), cos, sin, seq=S)
            y = _retention(q, k, v, gate, ret_gn_g[layer].reshape(1, -1), batch=B, seq=S)
            x2 = _out_proj(x2, y, w_ret_out[layer].astype(BF16))
        else:
            q, k_sb, v_sb = _sb_proj(x2, kv_norm_g.reshape(1, D), mix_g,
                                     w_kv.astype(BF16), w_sb_q[0].astype(BF16), batch=B, seq=S)
            o = _sb_attn(q, k_sb, v_sb)
            x2 = _sb_out(x2, o, w_sb_out[0].astype(BF16), seq=S)
        x2 = _ffn(x2, ffn_norm_g[layer].reshape(1, D), w_ffn_in[layer].astype(BF16),
                  w_ffn_out[layer].astype(BF16), final_g, final_norm=(layer == depth - 1))
    return x2.reshape(B, S, D)
```

```python
import functools

import jax
import jax.numpy as jnp
from jax import lax
from jax.experimental import pallas as pl
from jax.experimental.pallas import tpu as pltpu

RMS_EPS = 1e-6
GN_EPS = 1e-5
RET_HEADS = 4
RET_CHUNK = 128
ROPE_BASE = 10000.0
SB_HEADS = 16

LOG2_E = 1.4426950408889634

F32 = jnp.float32
BF16 = jnp.bfloat16

V7X_VMEM_BYTES = 64 * 1024 * 1024
VMEM_LIMIT_BYTES = (V7X_VMEM_BYTES * 7) // 8

NT_DIMS = (((1,), (1,)), ((), ()))


def _params(*semantics):
    return pltpu.CompilerParams(dimension_semantics=semantics, vmem_limit_bytes=VMEM_LIMIT_BYTES)


def _rms_scale(x):
    return x * lax.rsqrt(jnp.mean(x * x, axis=-1, keepdims=True) + RMS_EPS)


def _silu(g):
    return g * (1.0 / (1.0 + jnp.exp(-g)))


def _dot(a, b):
    return jnp.dot(a, b, preferred_element_type=F32)


def _ret_proj_kernel(x_ref, g_ref, w_ref, cos_ref, sin_ref, kdec_ref, q_ref, k_ref, kdt_ref, v_ref, gate_ref,
                     *, dk, k_scale):
    hb = (_rms_scale(x_ref[...]) * g_ref[...]).astype(BF16)
    cos = cos_ref[...]
    sin = sin_ref[...]
    half = dk // 2
    qk_width = q_ref.shape[1]
    v_width = v_ref.shape[1]
    for base, out_ref, scale in ((0, q_ref, 1.0), (qk_width, k_ref, k_scale)):
        for c in range(0, qk_width, dk):
            t = _dot(hb, w_ref[:, base + c:base + c + dk])
            t1, t2 = t[:, :half], t[:, half:]
            r1 = (t1 * cos - t2 * sin) * scale
            r2 = (t1 * sin + t2 * cos) * scale
            out_ref[:, c:c + half] = r1.astype(BF16)
            out_ref[:, c + half:c + dk] = r2.astype(BF16)
            if out_ref is k_ref:
                kdec = kdec_ref[c // dk]
                kdt_ref[c:c + half, :] = (r1 * kdec).T.astype(BF16)
                kdt_ref[c + half:c + dk, :] = (r2 * kdec).T.astype(BF16)
    for base, out_ref in ((2 * qk_width, v_ref), (2 * qk_width + v_width, gate_ref)):
        for c in range(0, v_width, 2 * dk):
            out_ref[:, c:c + 2 * dk] = _dot(hb, w_ref[:, base + c:base + c + 2 * dk]).astype(BF16)


def _ret_proj(x2, g, w, cos, sin, *, seq, tm=256):
    T, D = x2.shape
    dk = D // RET_HEADS
    qk_width, v_width = D, 2 * D
    sblocks = seq // tm
    row = lambda i: (i, 0)
    kdec = jnp.tile(_retention_decays()[2], (1, tm // RET_CHUNK, 1))
    return pl.pallas_call(
        functools.partial(_ret_proj_kernel, dk=dk, k_scale=dk ** -0.5),
        grid=(T // tm,),
        in_specs=[
            pl.BlockSpec((tm, D), row),
            pl.BlockSpec((1, D), lambda i: (0, 0)),
            pl.BlockSpec(w.shape, lambda i: (0, 0)),
            pl.BlockSpec((tm, dk // 2), lambda i: (i % sblocks, 0)),
            pl.BlockSpec((tm, dk // 2), lambda i: (i % sblocks, 0)),
            pl.BlockSpec(kdec.shape, lambda i: (0, 0, 0)),
        ],
        out_specs=[
            pl.BlockSpec((tm, qk_width), row),
            pl.BlockSpec((tm, qk_width), row),
            pl.BlockSpec((qk_width, tm), lambda i: (0, i)),
            pl.BlockSpec((tm, v_width), row),
            pl.BlockSpec((tm, v_width), row),
        ],
        out_shape=[
            jax.ShapeDtypeStruct((T, qk_width), BF16),
            jax.ShapeDtypeStruct((T, qk_width), BF16),
            jax.ShapeDtypeStruct((qk_width, T), BF16),
            jax.ShapeDtypeStruct((T, v_width), BF16),
            jax.ShapeDtypeStruct((T, v_width), BF16),
        ],
        compiler_params=_params("parallel"),
        name="ret_proj",
    )(x2, g, w, cos, sin, kdec)


def _retention_kernel(q_ref, k_ref, kdt_ref, v_ref, dmat_ref, qdec_ref, cdec_ref, o_ref, state_ref, *, chunk):
    @pl.when(pl.program_id(2) == 0)
    def _():
        state_ref[...] = jnp.zeros_like(state_ref)

    dmat = dmat_ref[0]
    qdec = qdec_ref[0]
    cdec = cdec_ref[0]
    n_chunks = q_ref.shape[0] // chunk
    chunks = [slice(c * chunk, (c + 1) * chunk) for c in range(n_chunks)]
    updates = [_dot(kdt_ref[:, rows], v_ref[rows, :]) for rows in chunks]
    states = [state_ref[...]]
    for upd in updates:
        states.append(states[-1] * cdec + upd)
    state_ref[...] = states[-1]
    for rows, state in zip(chunks, states):
        qc = q_ref[rows, :]
        vc = v_ref[rows, :]
        scores = lax.dot_general(qc, k_ref[rows, :], NT_DIMS, preferred_element_type=F32) * dmat
        o_ref[rows, :] = _dot(scores.astype(BF16), vc) + _dot(qc, state.astype(BF16)) * qdec


def _retention_decays():
    H, C = RET_HEADS, RET_CHUNK
    log_g = jnp.log(1.0 - 2.0 ** (-5.0 - jnp.arange(H, dtype=F32)))
    j = jnp.arange(C, dtype=F32)
    rel = j[:, None] - j[None, :]
    dmat = jnp.where(rel[None] >= 0, jnp.exp(rel[None] * log_g[:, None, None]), 0.0)
    qdec = jnp.exp((j[None, :] + 1.0) * log_g[:, None])[:, :, None]
    kdec = jnp.exp((C - 1.0 - j[None, :]) * log_g[:, None])[:, :, None]
    cdec = jnp.exp(C * log_g)[:, None, None]
    return dmat, qdec, kdec, cdec


def _retention(q, k, kdt, v, *, batch, seq, rows_per_step=512):
    T = q.shape[0]
    H, C = RET_HEADS, RET_CHUNK
    dk = q.shape[1] // H
    dv = v.shape[1] // H
    nsb = seq // rows_per_step
    dmat, qdec, _, cdec = _retention_decays()

    tok = lambda b, h, s: (b * nsb + s, h)
    per_head = lambda b, h, s: (h, 0, 0)
    return pl.pallas_call(
        functools.partial(_retention_kernel, chunk=C),
        grid=(batch, H, nsb),
        in_specs=[
            pl.BlockSpec((rows_per_step, dk), tok),
            pl.BlockSpec((rows_per_step, dk), tok),
            pl.BlockSpec((dk, rows_per_step), lambda b, h, s: (h, b * nsb + s)),
            pl.BlockSpec((rows_per_step, dv), tok),
            pl.BlockSpec((1, C, C), per_head),
            pl.BlockSpec((1, C, 1), per_head),
            pl.BlockSpec((1, 1, 1), per_head),
        ],
        out_specs=pl.BlockSpec((rows_per_step, dv), tok),
        out_shape=jax.ShapeDtypeStruct((T, v.shape[1]), F32),
        scratch_shapes=[pltpu.VMEM((dk, dv), F32)],
        compiler_params=_params("parallel", "parallel", "arbitrary"),
        name="retention",
    )(q, k, kdt, v, dmat, qdec, cdec)


def _ffn_tail(x1, g_ref, win_ref, wout_ref, fg_ref, out_ref, *, chunk, final_norm):
    d_ff = wout_ref.shape[0]
    hb = (_rms_scale(x1) * g_ref[...]).astype(BF16)
    out_ref[...] = x1
    for c in range(0, d_ff, chunk):
        gate = _dot(hb, win_ref[:, c:c + chunk])
        up = _dot(hb, win_ref[:, d_ff + c:d_ff + c + chunk])
        act = (_silu(gate) * up).astype(BF16)
        out_ref[...] += _dot(act, wout_ref[c:c + chunk, :])
    if final_norm:
        out_ref[...] = _rms_scale(out_ref[...]) * fg_ref[...]


def _ret_out_ffn_kernel(x_ref, o_ref, gate_ref, gn_ref, wo_ref, g_ref, win_ref, wout_ref, fg_ref, out_ref,
                        *, dv, chunk, final_norm):
    x1 = x_ref[...]
    for c in range(0, o_ref.shape[1], dv):
        o = o_ref[:, c:c + dv]
        oc = o - jnp.mean(o, axis=-1, keepdims=True)
        on = oc * lax.rsqrt(jnp.mean(oc * oc, axis=-1, keepdims=True) + GN_EPS) * gn_ref[:, c:c + dv]
        y = (_silu(gate_ref[:, c:c + dv].astype(F32)) * on).astype(BF16)
        x1 = x1 + _dot(y, wo_ref[c:c + dv, :])
    _ffn_tail(x1, g_ref, win_ref, wout_ref, fg_ref, out_ref, chunk=chunk, final_norm=final_norm)


def _sb_out_ffn_kernel(x_ref, o_ref, wo_ref, g_ref, win_ref, wout_ref, fg_ref, out_ref, *, chunk, final_norm):
    heads = jnp.concatenate([o_ref[0, h] for h in range(o_ref.shape[1])], axis=1)
    x1 = x_ref[...] + _dot(heads, wo_ref[...])
    _ffn_tail(x1, g_ref, win_ref, wout_ref, fg_ref, out_ref, chunk=chunk, final_norm=final_norm)


def _resident(shape):
    return pl.BlockSpec(shape, lambda i: (0,) * len(shape), pipeline_mode=pl.Buffered(1))


def _mixer_out_ffn(body, x2, mixer_specs, mixer_args, g, w_in, w_out, final_g, *, name, tm):
    T, D = x2.shape
    row = lambda i: (i, 0)
    return pl.pallas_call(
        body,
        grid=(T // tm,),
        in_specs=[pl.BlockSpec((tm, D), row)] + mixer_specs + [
            _resident((1, D)), _resident(w_in.shape), _resident(w_out.shape), _resident((1, D))],
        out_specs=pl.BlockSpec((tm, D), row),
        out_shape=jax.ShapeDtypeStruct((T, D), F32),
        compiler_params=_params("parallel"),
        name=name,
    )(x2, *mixer_args, g, w_in, w_out, final_g)


def _ret_out_ffn(x2, o, gate, gn_g, wo, g, w_in, w_out, final_g, *, final_norm, tm=512, chunk=256):
    width = o.shape[1]
    row = lambda i: (i, 0)
    body = functools.partial(_ret_out_ffn_kernel, dv=width // RET_HEADS, chunk=chunk, final_norm=final_norm)
    specs = [pl.BlockSpec((tm, width), row), pl.BlockSpec((tm, width), row), _resident((1, width)),
             _resident(wo.shape)]
    return _mixer_out_ffn(body, x2, specs, (o, gate, gn_g, wo), g, w_in, w_out, final_g,
                          name="ret_out_ffn", tm=tm)


def _sb_out_ffn(x2, o, wo, g, w_in, w_out, final_g, *, seq, final_norm, tm=512, chunk=256):
    _, H, _, dh = o.shape
    sblocks = seq // tm
    body = functools.partial(_sb_out_ffn_kernel, chunk=chunk, final_norm=final_norm)
    specs = [pl.BlockSpec((1, H, tm, dh), lambda i: (i // sblocks, 0, i % sblocks, 0)), _resident(wo.shape)]
    return _mixer_out_ffn(body, x2, specs, (o, wo), g, w_in, w_out, final_g, name="sb_out_ffn", tm=tm)


def _sb_proj_kernel(x_ref, gkv_ref, gq_ref, wkv_ref, wq_ref, q_ref, k_ref, vt_ref, *, dh, q_scale, group):
    r = _rms_scale(x_ref[...])
    hkv = (r * gkv_ref[...]).astype(BF16)
    hq = (r * gq_ref[...]).astype(BF16)
    D = wq_ref.shape[1]
    width = group * dh
    for c in range(0, D, width):
        tq = _dot(hq, wq_ref[:, c:c + width]) * q_scale
        tk = _dot(hkv, wkv_ref[:, c:c + width])
        tv = _dot(hkv, wkv_ref[:, D + c:D + c + width])
        for g in range(group):
            h = c // dh + g
            lanes = slice(g * dh, (g + 1) * dh)
            q_ref[0, h] = tq[:, lanes].astype(BF16)
            k_ref[0, h] = tk[:, lanes].astype(BF16)
            vt_ref[0, h] = tv[:, lanes].T.astype(BF16)


def _sb_proj(x2, gkv, gq, wkv, wq, *, batch, seq, tm=512):
    T, D = x2.shape
    H = SB_HEADS
    dh = D // H
    sblocks = seq // tm
    const = lambda i: (0, 0)
    head_major = pl.BlockSpec((1, H, tm, dh), lambda i: (i // sblocks, 0, i % sblocks, 0))
    shape = jax.ShapeDtypeStruct((batch, H, seq, dh), BF16)
    return pl.pallas_call(
        functools.partial(_sb_proj_kernel, dh=dh, q_scale=dh ** -0.5 * LOG2_E, group=4),
        grid=(T // tm,),
        in_specs=[
            pl.BlockSpec((tm, D), lambda i: (i, 0)),
            pl.BlockSpec((1, D), const),
            pl.BlockSpec((1, D), const),
            pl.BlockSpec(wkv.shape, const),
            pl.BlockSpec(wq.shape, const),
        ],
        out_specs=[head_major, head_major,
                   pl.BlockSpec((1, H, dh, tm), lambda i: (i // sblocks, 0, 0, i % sblocks))],
        out_shape=[shape, shape, jax.ShapeDtypeStruct((batch, H, dh, seq), BF16)],
        compiler_params=_params("parallel"),
        name="sb_proj",
    )(x2, gkv, gq, wkv, wq)


MASKED_LOG2 = -1e30


SB_STAGES = ("dots", "soft", "suffix", "expo", "accum")
SB_STAGE_OFFSETS = (0, 1, 2, 3, 4)
SB_TABLE_FIELDS = 3


def _sb_tail(depth):
    return depth + depth % 2


def _sb_streams(seq, tq, tk, depth):
    band = tq // tk
    streams = [[], []]
    for qt in range(seq // tq):
        q0 = qt * tq
        for n in range(band + band * qt):
            streams[0 if n < band else 1].append((q0, q0 + (band - 1 - n) * tk, q0))
    counts = [len(s) for s in streams]
    flat, bases = [], []
    for s in streams:
        s = s + [(0, 0, seq)] * _sb_tail(depth)
        bases.append(len(flat))
        for field in range(SB_TABLE_FIELDS):
            flat.extend(e[field] for e in s)
    return counts, bases, jnp.asarray(flat, jnp.int32)


def _sb_attn_kernel(tab_ref, q_ref, k_ref, vt_ref, tri_ref, o_ref, z_ref, sp_ref, lb_ref, tot_ref, m_ref,
                    a_ref, acct_ref, ls_ref, *, tq, tk, offsets, counts, bases, unroll):
    seq = q_ref.shape[2]
    lanes = tot_ref.shape[2]
    band = tq // tk
    off = dict(zip(SB_STAGES, offsets))
    depth = max(offsets)
    assert list(offsets) == sorted(offsets) and offsets[0] == 0 and band in (1, 2)
    sign_bit = jnp.uint32(0x80000000)

    def handoff(ref, slot, producer, consumer, value=None):
        gap = off[consumer] - off[producer]
        assert 0 <= gap <= 2
        return value if gap == 0 else ref[(slot + gap) % 2]

    def run_phase(p, parity, base, length, masked, active):
        slot = parity
        out = {}

        def field(f, e):
            return tab_ref[base + f * length + e]

        for s in sorted(SB_STAGES, key=lambda s: (-off[s], SB_STAGES.index(s))):
            o = off[s]
            if active(s):
                e = p - o
                if s == "dots":
                    qrow = pl.multiple_of(field(0, e), tq)
                    key0 = pl.multiple_of(field(1, e), tk)
                    z = lax.dot_general(q_ref[0, 0, pl.ds(qrow, tq), :], k_ref[0, 0, pl.ds(key0, tk), :],
                                        NT_DIMS, preferred_element_type=F32)
                    out["z"] = z
                    if off["soft"] > o:
                        z_ref[slot] = z
                elif s == "soft":
                    z = handoff(z_ref, slot, "dots", "soft", out.get("z"))
                    neg_abs = pltpu.bitcast(pltpu.bitcast(z, jnp.uint32) | sign_bit, F32)
                    sp = jnp.maximum(z, 0.0) + jnp.log2(1.0 + jnp.exp2(neg_abs))
                    lb = z - sp
                    if masked:
                        diag_shift = (band - 1 - (parity - o) % band) * tk
                        valid = (lax.broadcasted_iota(jnp.int32, (tq, tk), 1) + diag_shift
                                 < lax.broadcasted_iota(jnp.int32, (tq, tk), 0))
                        sp = jnp.where(valid, sp, 0.0)
                        lb = jnp.where(valid, lb, MASKED_LOG2)
                    out["sp"] = sp.astype(BF16)
                    out["lb"] = lb
                    out["tot"] = jnp.broadcast_to(jnp.sum(sp, axis=-1, keepdims=True), (tq, lanes))
                    if off["suffix"] > o:
                        sp_ref[slot] = out["sp"]
                        lb_ref[slot] = lb
                    if off["expo"] > o:
                        tot_ref[slot] = out["tot"]
                elif s == "suffix":
                    sp_b = handoff(sp_ref, slot, "soft", "suffix", out.get("sp"))
                    lb = handoff(lb_ref, slot, "soft", "suffix", out.get("lb"))
                    out["m"] = lb - _dot(sp_b, tri_ref[...])
                    if off["expo"] > o:
                        m_ref[slot] = out["m"]
                elif s == "expo":
                    m = handoff(m_ref, slot, "suffix", "expo", out.get("m"))
                    tot = handoff(tot_ref, slot, "soft", "expo", out.get("tot"))
                    rows = pl.ds(pl.multiple_of(field(2, e), tq), tq)
                    ls = ls_ref[rows, :]
                    ls_ref[rows, :] = ls + tot
                    a = jnp.exp2(m - jnp.concatenate([ls] * (tk // lanes), axis=1))
                    out["a"] = a.astype(BF16)
                    if off["accum"] > o:
                        a_ref[slot] = out["a"]
                else:
                    a_b = handoff(a_ref, slot, "expo", "accum", out.get("a"))
                    key0 = pl.multiple_of(field(1, e), tk)
                    rows = pl.ds(pl.multiple_of(field(2, e), tq), tq)
                    acct_ref[:, rows] += lax.dot_general(vt_ref[0, 0, :, pl.ds(key0, tk)], a_b, NT_DIMS,
                                                         preferred_element_type=F32)

    def run_stream(index, masked, unroll):
        count, base = counts[index], bases[index]
        fill = _sb_tail(depth)
        length = count + fill
        while count % unroll:
            unroll //= 2
        assert unroll % 2 == 0
        for p in range(fill):
            run_phase(p, p % 2, base, length, masked, lambda s: p - off[s] >= 0)

        def steady(i, c):
            for j in range(unroll):
                run_phase(fill + unroll * i + j, j % 2, base, length, masked, lambda s: True)
            return c

        lax.fori_loop(0, (length - fill) // unroll, steady, 0)

    acct_ref[...] = jnp.zeros_like(acct_ref)
    ls_ref[...] = jnp.zeros_like(ls_ref)
    run_stream(0, True, min(unroll, 8))
    run_stream(1, False, unroll)

    def write_out(i, c):
        rows = pl.ds(pl.multiple_of(i * tq, tq), tq)
        o_ref[0, 0, rows, :] = acct_ref[:, rows].T.astype(o_ref.dtype)
        return c

    lax.fori_loop(0, seq // tq, write_out, 0)


def _sb_attn(q, k, vt, *, tq=256, tk=256, lanes=128, offsets=SB_STAGE_OFFSETS, unroll=16):
    B, H, S, dh = q.shape
    idx = jnp.arange(tk)
    tri = (idx[:, None] > idx[None, :]).astype(BF16)
    counts, bases, table = _sb_streams(S, tq, tk, max(offsets))
    per_head = pl.BlockSpec((1, 1, S, dh), lambda b, h, tab: (b, h, 0, 0))
    return pl.pallas_call(
        functools.partial(_sb_attn_kernel, tq=tq, tk=tk, offsets=offsets, counts=counts, bases=bases,
                          unroll=unroll),
        grid_spec=pltpu.PrefetchScalarGridSpec(
            num_scalar_prefetch=1,
            grid=(B, H),
            in_specs=[per_head, per_head, pl.BlockSpec((1, 1, dh, S), lambda b, h, tab: (b, h, 0, 0)),
                      pl.BlockSpec(tri.shape, lambda b, h, tab: (0, 0))],
            out_specs=per_head,
            scratch_shapes=[
                pltpu.VMEM((2, tq, tk), F32),
                pltpu.VMEM((2, tq, tk), BF16),
                pltpu.VMEM((2, tq, tk), F32),
                pltpu.VMEM((2, tq, lanes), F32),
                pltpu.VMEM((2, tq, tk), F32),
                pltpu.VMEM((2, tq, tk), BF16),
                pltpu.VMEM((dh, S + tq), F32),
                pltpu.VMEM((S + tq, lanes), F32),
            ],
        ),
        out_shape=jax.ShapeDtypeStruct((B, H, S, dh), BF16),
        compiler_params=_params("parallel", "parallel"),
        name="sb_attn",
    )(table, q, k, vt, tri)


def _rotary_tables(seq, dk):
    half = dk // 2
    inv_freq = ROPE_BASE ** (-jnp.arange(half, dtype=F32) / half)
    ang = jnp.arange(seq, dtype=F32)[:, None] * inv_freq[None, :]
    return jnp.cos(ang), jnp.sin(ang)


def kernel(x, mix_norm_g, w_ret_in, ret_gn_g, w_ret_out, kv_norm_g, w_kv, w_sb_q, w_sb_out,
           ffn_norm_g, w_ffn_in, w_ffn_out, final_norm_g):
    B, S, D = x.shape
    depth = mix_norm_g.shape[0]
    n_ret = w_ret_in.shape[0]
    assert w_sb_q.shape[0] == 1 and depth == n_ret + 1
    x2 = x.reshape(B * S, D)
    cos, sin = _rotary_tables(S, D // RET_HEADS)
    final_g = final_norm_g.reshape(1, D)

    for layer in range(depth):
        mix_g = mix_norm_g[layer].reshape(1, D)
        ffn_args = (ffn_norm_g[layer].reshape(1, D), w_ffn_in[layer].astype(BF16),
                    w_ffn_out[layer].astype(BF16), final_g)
        final_norm = layer == depth - 1
        if layer < n_ret:
            q, k, kdt, v, gate = _ret_proj(x2, mix_g, w_ret_in[layer].astype(BF16), cos, sin, seq=S)
            o = _retention(q, k, kdt, v, batch=B, seq=S)
            x2 = _ret_out_ffn(x2, o, gate, ret_gn_g[layer].reshape(1, -1), w_ret_out[layer].astype(BF16),
                              *ffn_args, final_norm=final_norm)
        else:
            q, k_sb, vt_sb = _sb_proj(x2, kv_norm_g.reshape(1, D), mix_g,
                                     w_kv.astype(BF16), w_sb_q[0].astype(BF16), batch=B, seq=S)
            o = _sb_attn(q, k_sb, vt_sb)
            x2 = _sb_out_ffn(x2, o, w_sb_out[0].astype(BF16), *ffn_args, seq=S, final_norm=final_norm)
    return x2.reshape(B, S, D)
```

```python
import functools

import jax
import jax.numpy as jnp
from jax import lax
from jax.experimental import pallas as pl
from jax.experimental.pallas import tpu as pltpu

RMS_EPS = 1e-6
GN_EPS = 1e-5
RET_HEADS = 4
RET_CHUNK = 128
ROPE_BASE = 10000.0
SB_HEADS = 16

LOG2_E = 1.4426950408889634

F32 = jnp.float32
BF16 = jnp.bfloat16

V7X_VMEM_BYTES = 64 * 1024 * 1024
VMEM_LIMIT_BYTES = (V7X_VMEM_BYTES * 7) // 8

NT_DIMS = (((1,), (1,)), ((), ()))


def _params(*semantics):
    return pltpu.CompilerParams(dimension_semantics=semantics, vmem_limit_bytes=VMEM_LIMIT_BYTES)


def _rms_scale(x):
    return x * lax.rsqrt(jnp.mean(x * x, axis=-1, keepdims=True) + RMS_EPS)


def _silu(g):
    return g * (1.0 / (1.0 + jnp.exp(-g)))


def _dot(a, b):
    return jnp.dot(a, b, preferred_element_type=F32)


def _ret_proj_kernel(x_ref, g_ref, w_ref, cos_ref, sin_ref, kdec_ref, q_ref, k_ref, kdt_ref, v_ref, gate_ref,
                     *, dk, k_scale):
    hb = (_rms_scale(x_ref[...]) * g_ref[...]).astype(BF16)
    cos = cos_ref[...]
    sin = sin_ref[...]
    half = dk // 2
    qk_width = q_ref.shape[1]
    v_width = v_ref.shape[1]
    for base, out_ref, scale in ((0, q_ref, 1.0), (qk_width, k_ref, k_scale)):
        for c in range(0, qk_width, dk):
            t = _dot(hb, w_ref[:, base + c:base + c + dk])
            t1, t2 = t[:, :half], t[:, half:]
            r1 = (t1 * cos - t2 * sin) * scale
            r2 = (t1 * sin + t2 * cos) * scale
            out_ref[:, c:c + half] = r1.astype(BF16)
            out_ref[:, c + half:c + dk] = r2.astype(BF16)
            if out_ref is k_ref:
                kdec = kdec_ref[c // dk]
                kdt_ref[c:c + half, :] = (r1 * kdec).T.astype(BF16)
                kdt_ref[c + half:c + dk, :] = (r2 * kdec).T.astype(BF16)
    for base, out_ref in ((2 * qk_width, v_ref), (2 * qk_width + v_width, gate_ref)):
        for c in range(0, v_width, 2 * dk):
            out_ref[:, c:c + 2 * dk] = _dot(hb, w_ref[:, base + c:base + c + 2 * dk]).astype(BF16)


def _ret_proj(x2, g, w, cos, sin, *, seq, tm=256):
    T, D = x2.shape
    dk = D // RET_HEADS
    qk_width, v_width = D, 2 * D
    sblocks = seq // tm
    row = lambda i: (i, 0)
    kdec = jnp.tile(_retention_decays()[2], (1, tm // RET_CHUNK, 1))
    return pl.pallas_call(
        functools.partial(_ret_proj_kernel, dk=dk, k_scale=dk ** -0.5),
        grid=(T // tm,),
        in_specs=[
            pl.BlockSpec((tm, D), row),
            pl.BlockSpec((1, D), lambda i: (0, 0)),
            pl.BlockSpec(w.shape, lambda i: (0, 0)),
            pl.BlockSpec((tm, dk // 2), lambda i: (i % sblocks, 0)),
            pl.BlockSpec((tm, dk // 2), lambda i: (i % sblocks, 0)),
            pl.BlockSpec(kdec.shape, lambda i: (0, 0, 0)),
        ],
        out_specs=[
            pl.BlockSpec((tm, qk_width), row),
            pl.BlockSpec((tm, qk_width), row),
            pl.BlockSpec((qk_width, tm), lambda i: (0, i)),
            pl.BlockSpec((tm, v_width), row),
            pl.BlockSpec((tm, v_width), row),
        ],
        out_shape=[
            jax.ShapeDtypeStruct((T, qk_width), BF16),
            jax.ShapeDtypeStruct((T, qk_width), BF16),
            jax.ShapeDtypeStruct((qk_width, T), BF16),
            jax.ShapeDtypeStruct((T, v_width), BF16),
            jax.ShapeDtypeStruct((T, v_width), BF16),
        ],
        compiler_params=_params("parallel"),
        name="ret_proj",
    )(x2, g, w, cos, sin, kdec)


def _retention_kernel(q_ref, k_ref, kdt_ref, v_ref, dmat_ref, qdec_ref, cdec_ref, o_ref, state_ref, *, chunk):
    @pl.when(pl.program_id(2) == 0)
    def _():
        state_ref[...] = jnp.zeros_like(state_ref)

    dmat = dmat_ref[0]
    qdec = qdec_ref[0]
    cdec = cdec_ref[0]
    n_chunks = q_ref.shape[0] // chunk
    chunks = [slice(c * chunk, (c + 1) * chunk) for c in range(n_chunks)]
    updates = [_dot(kdt_ref[:, rows], v_ref[rows, :]) for rows in chunks]
    states = [state_ref[...]]
    for upd in updates:
        states.append(states[-1] * cdec + upd)
    state_ref[...] = states[-1]
    for rows, state in zip(chunks, states):
        qc = q_ref[rows, :]
        vc = v_ref[rows, :]
        scores = lax.dot_general(qc, k_ref[rows, :], NT_DIMS, preferred_element_type=F32) * dmat
        o_ref[rows, :] = _dot(scores.astype(BF16), vc) + _dot(qc, state.astype(BF16)) * qdec


def _retention_decays():
    H, C = RET_HEADS, RET_CHUNK
    log_g = jnp.log(1.0 - 2.0 ** (-5.0 - jnp.arange(H, dtype=F32)))
    j = jnp.arange(C, dtype=F32)
    rel = j[:, None] - j[None, :]
    dmat = jnp.where(rel[None] >= 0, jnp.exp(rel[None] * log_g[:, None, None]), 0.0)
    qdec = jnp.exp((j[None, :] + 1.0) * log_g[:, None])[:, :, None]
    kdec = jnp.exp((C - 1.0 - j[None, :]) * log_g[:, None])[:, :, None]
    cdec = jnp.exp(C * log_g)[:, None, None]
    return dmat, qdec, kdec, cdec


def _retention(q, k, kdt, v, *, batch, seq, rows_per_step=2048):
    T = q.shape[0]
    H, C = RET_HEADS, RET_CHUNK
    dk = q.shape[1] // H
    dv = v.shape[1] // H
    nsb = seq // rows_per_step
    dmat, qdec, _, cdec = _retention_decays()

    tok = lambda b, h, s: (b * nsb + s, h)
    per_head = lambda b, h, s: (h, 0, 0)
    return pl.pallas_call(
        functools.partial(_retention_kernel, chunk=C),
        grid=(batch, H, nsb),
        in_specs=[
            pl.BlockSpec((rows_per_step, dk), tok),
            pl.BlockSpec((rows_per_step, dk), tok),
            pl.BlockSpec((dk, rows_per_step), lambda b, h, s: (h, b * nsb + s)),
            pl.BlockSpec((rows_per_step, dv), tok),
            pl.BlockSpec((1, C, C), per_head),
            pl.BlockSpec((1, C, 1), per_head),
            pl.BlockSpec((1, 1, 1), per_head),
        ],
        out_specs=pl.BlockSpec((rows_per_step, dv), tok),
        out_shape=jax.ShapeDtypeStruct((T, v.shape[1]), F32),
        scratch_shapes=[pltpu.VMEM((dk, dv), F32)],
        compiler_params=_params("parallel", "parallel", "arbitrary"),
        name="retention",
    )(q, k, kdt, v, dmat, qdec, cdec)


def _ffn_tail(x1, g_ref, win_ref, wout_ref, fg_ref, out_ref, *, chunk, final_norm):
    d_ff = wout_ref.shape[0]
    hb = (_rms_scale(x1) * g_ref[...]).astype(BF16)
    out_ref[...] = x1
    for c in range(0, d_ff, chunk):
        gate = _dot(hb, win_ref[:, c:c + chunk])
        up = _dot(hb, win_ref[:, d_ff + c:d_ff + c + chunk])
        act = (_silu(gate) * up).astype(BF16)
        out_ref[...] += _dot(act, wout_ref[c:c + chunk, :])
    if final_norm:
        out_ref[...] = _rms_scale(out_ref[...]) * fg_ref[...]


def _ret_out_ffn_kernel(x_ref, o_ref, gate_ref, gn_ref, wo_ref, g_ref, win_ref, wout_ref, fg_ref, out_ref,
                        *, dv, chunk, final_norm):
    x1 = x_ref[...]
    for c in range(0, o_ref.shape[1], dv):
        o = o_ref[:, c:c + dv]
        oc = o - jnp.mean(o, axis=-1, keepdims=True)
        on = oc * lax.rsqrt(jnp.mean(oc * oc, axis=-1, keepdims=True) + GN_EPS) * gn_ref[:, c:c + dv]
        y = (_silu(gate_ref[:, c:c + dv].astype(F32)) * on).astype(BF16)
        x1 = x1 + _dot(y, wo_ref[c:c + dv, :])
    _ffn_tail(x1, g_ref, win_ref, wout_ref, fg_ref, out_ref, chunk=chunk, final_norm=final_norm)


def _sb_out_ffn_kernel(x_ref, o_ref, wo_ref, g_ref, win_ref, wout_ref, fg_ref, out_ref, *, chunk, final_norm):
    heads = jnp.concatenate([o_ref[0, h] for h in range(o_ref.shape[1])], axis=1)
    x1 = x_ref[...] + _dot(heads, wo_ref[...])
    _ffn_tail(x1, g_ref, win_ref, wout_ref, fg_ref, out_ref, chunk=chunk, final_norm=final_norm)


def _resident(shape):
    return pl.BlockSpec(shape, lambda i: (0,) * len(shape), pipeline_mode=pl.Buffered(1))


def _mixer_out_ffn(body, x2, mixer_specs, mixer_args, g, w_in, w_out, final_g, *, name, tm):
    T, D = x2.shape
    row = lambda i: (i, 0)
    return pl.pallas_call(
        body,
        grid=(T // tm,),
        in_specs=[pl.BlockSpec((tm, D), row)] + mixer_specs + [
            _resident((1, D)), _resident(w_in.shape), _resident(w_out.shape), _resident((1, D))],
        out_specs=pl.BlockSpec((tm, D), row),
        out_shape=jax.ShapeDtypeStruct((T, D), F32),
        compiler_params=_params("parallel"),
        name=name,
    )(x2, *mixer_args, g, w_in, w_out, final_g)


def _ret_out_ffn(x2, o, gate, gn_g, wo, g, w_in, w_out, final_g, *, final_norm, tm=512, chunk=256):
    width = o.shape[1]
    row = lambda i: (i, 0)
    body = functools.partial(_ret_out_ffn_kernel, dv=width // RET_HEADS, chunk=chunk, final_norm=final_norm)
    specs = [pl.BlockSpec((tm, width), row), pl.BlockSpec((tm, width), row), _resident((1, width)),
             _resident(wo.shape)]
    return _mixer_out_ffn(body, x2, specs, (o, gate, gn_g, wo), g, w_in, w_out, final_g,
                          name="ret_out_ffn", tm=tm)


def _sb_out_ffn(x2, o, wo, g, w_in, w_out, final_g, *, seq, final_norm, tm=512, chunk=256):
    _, H, _, dh = o.shape
    sblocks = seq // tm
    body = functools.partial(_sb_out_ffn_kernel, chunk=chunk, final_norm=final_norm)
    specs = [pl.BlockSpec((1, H, tm, dh), lambda i: (i // sblocks, 0, i % sblocks, 0)), _resident(wo.shape)]
    return _mixer_out_ffn(body, x2, specs, (o, wo), g, w_in, w_out, final_g, name="sb_out_ffn", tm=tm)


def _sb_proj_kernel(x_ref, gkv_ref, gq_ref, wkv_ref, wq_ref, q_ref, k_ref, vt_ref, *, dh, q_scale, group):
    r = _rms_scale(x_ref[...])
    hkv = (r * gkv_ref[...]).astype(BF16)
    hq = (r * gq_ref[...]).astype(BF16)
    D = wq_ref.shape[1]
    width = group * dh
    for c in range(0, D, width):
        tq = _dot(hq, wq_ref[:, c:c + width]) * q_scale
        tk = _dot(hkv, wkv_ref[:, c:c + width])
        tv = _dot(hkv, wkv_ref[:, D + c:D + c + width])
        for g in range(group):
            h = c // dh + g
            lanes = slice(g * dh, (g + 1) * dh)
            q_ref[0, h] = tq[:, lanes].astype(BF16)
            k_ref[0, h] = tk[:, lanes].astype(BF16)
            vt_ref[0, h] = tv[:, lanes].T.astype(BF16)


def _sb_proj(x2, gkv, gq, wkv, wq, *, batch, seq, tm=512):
    T, D = x2.shape
    H = SB_HEADS
    dh = D // H
    sblocks = seq // tm
    const = lambda i: (0, 0)
    head_major = pl.BlockSpec((1, H, tm, dh), lambda i: (i // sblocks, 0, i % sblocks, 0))
    shape = jax.ShapeDtypeStruct((batch, H, seq, dh), BF16)
    return pl.pallas_call(
        functools.partial(_sb_proj_kernel, dh=dh, q_scale=dh ** -0.5 * LOG2_E, group=4),
        grid=(T // tm,),
        in_specs=[
            pl.BlockSpec((tm, D), lambda i: (i, 0)),
            pl.BlockSpec((1, D), const),
            pl.BlockSpec((1, D), const),
            pl.BlockSpec(wkv.shape, const),
            pl.BlockSpec(wq.shape, const),
        ],
        out_specs=[head_major, head_major,
                   pl.BlockSpec((1, H, dh, tm), lambda i: (i // sblocks, 0, 0, i % sblocks))],
        out_shape=[shape, shape, jax.ShapeDtypeStruct((batch, H, dh, seq), BF16)],
        compiler_params=_params("parallel"),
        name="sb_proj",
    )(x2, gkv, gq, wkv, wq)


MASKED_LOG2 = -1e30


SB_STAGES = ("dots", "soft", "suffix", "expo", "accum")
SB_STAGE_OFFSETS = (0, 1, 2, 3, 4)
SB_TABLE_FIELDS = 3


def _sb_tail(depth):
    return depth + depth % 2


def _sb_streams(seq, tq, tk, depth):
    band = tq // tk
    streams = [[], []]
    for qt in range(seq // tq):
        q0 = qt * tq
        for n in range(band + band * qt):
            streams[0 if n < band else 1].append((q0, q0 + (band - 1 - n) * tk, q0))
    counts = [len(s) for s in streams]
    flat, bases = [], []
    for s in streams:
        s = s + [(0, 0, seq)] * _sb_tail(depth)
        bases.append(len(flat))
        for field in range(SB_TABLE_FIELDS):
            flat.extend(e[field] for e in s)
    return counts, bases, jnp.asarray(flat, jnp.int32)


def _sb_attn_kernel(tab_ref, q_ref, k_ref, vt_ref, tri_ref, o_ref, z_ref, sp_ref, lb_ref, tot_ref, m_ref,
                    a_ref, acct_ref, ls_ref, *, tq, tk, offsets, counts, bases, unroll):
    seq = q_ref.shape[2]
    lanes = tot_ref.shape[2]
    band = tq // tk
    off = dict(zip(SB_STAGES, offsets))
    depth = max(offsets)
    assert list(offsets) == sorted(offsets) and offsets[0] == 0 and band in (1, 2)
    sign_bit = jnp.uint32(0x80000000)

    def handoff(ref, slot, producer, consumer, value=None):
        gap = off[consumer] - off[producer]
        assert 0 <= gap <= 2
        return value if gap == 0 else ref[(slot + gap) % 2]

    def run_phase(p, parity, base, length, masked, active):
        slot = parity
        out = {}

        def field(f, e):
            return tab_ref[base + f * length + e]

        for s in sorted(SB_STAGES, key=lambda s: (-off[s], SB_STAGES.index(s))):
            o = off[s]
            if active(s):
                e = p - o
                if s == "dots":
                    qrow = pl.multiple_of(field(0, e), tq)
                    key0 = pl.multiple_of(field(1, e), tk)
                    z = lax.dot_general(q_ref[0, 0, pl.ds(qrow, tq), :], k_ref[0, 0, pl.ds(key0, tk), :],
                                        NT_DIMS, preferred_element_type=F32)
                    out["z"] = z
                    if off["soft"] > o:
                        z_ref[slot] = z
                elif s == "soft":
                    z = handoff(z_ref, slot, "dots", "soft", out.get("z"))
                    neg_abs = pltpu.bitcast(pltpu.bitcast(z, jnp.uint32) | sign_bit, F32)
                    sp = jnp.maximum(z, 0.0) + jnp.log2(1.0 + jnp.exp2(neg_abs))
                    lb = z - sp
                    if masked:
                        diag_shift = (band - 1 - (parity - o) % band) * tk
                        valid = (lax.broadcasted_iota(jnp.int32, (tq, tk), 1) + diag_shift
                                 < lax.broadcasted_iota(jnp.int32, (tq, tk), 0))
                        sp = jnp.where(valid, sp, 0.0)
                        lb = jnp.where(valid, lb, MASKED_LOG2)
                    out["sp"] = sp.astype(BF16)
                    out["lb"] = lb
                    out["tot"] = jnp.broadcast_to(jnp.sum(sp, axis=-1, keepdims=True), (tq, lanes))
                    if off["suffix"] > o:
                        sp_ref[slot] = out["sp"]
                        lb_ref[slot] = lb
                    if off["expo"] > o:
                        tot_ref[slot] = out["tot"]
                elif s == "suffix":
                    sp_b = handoff(sp_ref, slot, "soft", "suffix", out.get("sp"))
                    lb = handoff(lb_ref, slot, "soft", "suffix", out.get("lb"))
                    out["m"] = lb - _dot(sp_b, tri_ref[...])
                    if off["expo"] > o:
                        m_ref[slot] = out["m"]
                elif s == "expo":
                    m = handoff(m_ref, slot, "suffix", "expo", out.get("m"))
                    tot = handoff(tot_ref, slot, "soft", "expo", out.get("tot"))
                    rows = pl.ds(pl.multiple_of(field(2, e), tq), tq)
                    ls = ls_ref[rows, :]
                    ls_ref[rows, :] = ls + tot
                    a = jnp.exp2(m - jnp.concatenate([ls] * (tk // lanes), axis=1))
                    out["a"] = a.astype(BF16)
                    if off["accum"] > o:
                        a_ref[slot] = out["a"]
                else:
                    a_b = handoff(a_ref, slot, "expo", "accum", out.get("a"))
                    key0 = pl.multiple_of(field(1, e), tk)
                    rows = pl.ds(pl.multiple_of(field(2, e), tq), tq)
                    acct_ref[:, rows] += lax.dot_general(vt_ref[0, 0, :, pl.ds(key0, tk)], a_b, NT_DIMS,
                                                         preferred_element_type=F32)

    def run_stream(index, masked, unroll):
        count, base = counts[index], bases[index]
        fill = _sb_tail(depth)
        length = count + fill
        while count % unroll:
            unroll //= 2
        assert unroll % 2 == 0
        for p in range(fill):
            run_phase(p, p % 2, base, length, masked, lambda s: p - off[s] >= 0)

        def steady(i, c):
            for j in range(unroll):
                run_phase(fill + unroll * i + j, j % 2, base, length, masked, lambda s: True)
            return c

        lax.fori_loop(0, (length - fill) // unroll, steady, 0)

    acct_ref[...] = jnp.zeros_like(acct_ref)
    ls_ref[...] = jnp.zeros_like(ls_ref)
    run_stream(0, True, min(unroll, 8))
    run_stream(1, False, unroll)

    def write_out(i, c):
        rows = pl.ds(pl.multiple_of(i * tq, tq), tq)
        o_ref[0, 0, rows, :] = acct_ref[:, rows].T.astype(o_ref.dtype)
        return c

    lax.fori_loop(0, seq // tq, write_out, 0)


def _sb_attn(q, k, vt, *, tq=256, tk=256, lanes=128, offsets=SB_STAGE_OFFSETS, unroll=62):
    B, H, S, dh = q.shape
    idx = jnp.arange(tk)
    tri = (idx[:, None] > idx[None, :]).astype(BF16)
    counts, bases, table = _sb_streams(S, tq, tk, max(offsets))
    per_head = pl.BlockSpec((1, 1, S, dh), lambda b, h, tab: (b, h, 0, 0))
    return pl.pallas_call(
        functools.partial(_sb_attn_kernel, tq=tq, tk=tk, offsets=offsets, counts=counts, bases=bases,
                          unroll=unroll),
        grid_spec=pltpu.PrefetchScalarGridSpec(
            num_scalar_prefetch=1,
            grid=(B, H),
            in_specs=[per_head, per_head, pl.BlockSpec((1, 1, dh, S), lambda b, h, tab: (b, h, 0, 0)),
                      pl.BlockSpec(tri.shape, lambda b, h, tab: (0, 0))],
            out_specs=per_head,
            scratch_shapes=[
                pltpu.VMEM((2, tq, tk), F32),
                pltpu.VMEM((2, tq, tk), BF16),
                pltpu.VMEM((2, tq, tk), F32),
                pltpu.VMEM((2, tq, lanes), F32),
                pltpu.VMEM((2, tq, tk), F32),
                pltpu.VMEM((2, tq, tk), BF16),
                pltpu.VMEM((dh, S + tq), F32),
                pltpu.VMEM((S + tq, lanes), F32),
            ],
        ),
        out_shape=jax.ShapeDtypeStruct((B, H, S, dh), BF16),
        compiler_params=_params("parallel", "parallel"),
        name="sb_attn",
    )(table, q, k, vt, tri)


def _rotary_tables(seq, dk):
    half = dk // 2
    inv_freq = ROPE_BASE ** (-jnp.arange(half, dtype=F32) / half)
    ang = jnp.arange(seq, dtype=F32)[:, None] * inv_freq[None, :]
    return jnp.cos(ang), jnp.sin(ang)


def kernel(x, mix_norm_g, w_ret_in, ret_gn_g, w_ret_out, kv_norm_g, w_kv, w_sb_q, w_sb_out,
           ffn_norm_g, w_ffn_in, w_ffn_out, final_norm_g):
    B, S, D = x.shape
    depth = mix_norm_g.shape[0]
    n_ret = w_ret_in.shape[0]
    assert w_sb_q.shape[0] == 1 and depth == n_ret + 1
    x2 = x.reshape(B * S, D)
    cos, sin = _rotary_tables(S, D // RET_HEADS)
    final_g = final_norm_g.reshape(1, D)

    for layer in range(depth):
        mix_g = mix_norm_g[layer].reshape(1, D)
        ffn_args = (ffn_norm_g[layer].reshape(1, D), w_ffn_in[layer].astype(BF16),
                    w_ffn_out[layer].astype(BF16), final_g)
        final_norm = layer == depth - 1
        if layer < n_ret:
            q, k, kdt, v, gate = _ret_proj(x2, mix_g, w_ret_in[layer].astype(BF16), cos, sin, seq=S)
            o = _retention(q, k, kdt, v, batch=B, seq=S)
            x2 = _ret_out_ffn(x2, o, gate, ret_gn_g[layer].reshape(1, -1), w_ret_out[layer].astype(BF16),
                              *ffn_args, final_norm=final_norm)
        else:
            q, k_sb, vt_sb = _sb_proj(x2, kv_norm_g.reshape(1, D), mix_g,
                                     w_kv.astype(BF16), w_sb_q[0].astype(BF16), batch=B, seq=S)
            o = _sb_attn(q, k_sb, vt_sb)
            x2 = _sb_out_ffn(x2, o, w_sb_out[0].astype(BF16), *ffn_args, seq=S, final_norm=final_norm)
    return x2.reshape(B, S, D)
```

```python
import functools

import jax
import jax.numpy as jnp
from jax import lax
from jax.experimental import pallas as pl
from jax.experimental.pallas import tpu as pltpu

RMS_EPS = 1e-6
GN_EPS = 1e-5
RET_HEADS = 4
RET_CHUNK = 128
ROPE_BASE = 10000.0
SB_HEADS = 16

LOG2_E = 1.4426950408889634

F32 = jnp.float32
BF16 = jnp.bfloat16

V7X_VMEM_BYTES = 64 * 1024 * 1024
VMEM_LIMIT_BYTES = (V7X_VMEM_BYTES * 7) // 8

NT_DIMS = (((1,), (1,)), ((), ()))


def _params(*semantics):
    return pltpu.CompilerParams(dimension_semantics=semantics, vmem_limit_bytes=VMEM_LIMIT_BYTES)


def _rms_scale(x):
    return x * lax.rsqrt(jnp.mean(x * x, axis=-1, keepdims=True) + RMS_EPS)


def _silu(g):
    return g * (1.0 / (1.0 + jnp.exp(-g)))


def _dot(a, b):
    return jnp.dot(a, b, preferred_element_type=F32)


def _ret_proj_kernel(x_ref, g_ref, w_ref, cos_ref, sin_ref, kdec_ref, q_ref, k_ref, kdt_ref, v_ref, gate_ref,
                     *, dk, k_scale):
    hb = (_rms_scale(x_ref[...]) * g_ref[...]).astype(BF16)
    cos = cos_ref[...]
    sin = sin_ref[...]
    half = dk // 2
    qk_width = q_ref.shape[1]
    v_width = v_ref.shape[1]
    for base, out_ref, scale in ((0, q_ref, 1.0), (qk_width, k_ref, k_scale)):
        for c in range(0, qk_width, dk):
            t = _dot(hb, w_ref[:, base + c:base + c + dk])
            t1, t2 = t[:, :half], t[:, half:]
            r1 = (t1 * cos - t2 * sin) * scale
            r2 = (t1 * sin + t2 * cos) * scale
            out_ref[:, c:c + half] = r1.astype(BF16)
            out_ref[:, c + half:c + dk] = r2.astype(BF16)
            if out_ref is k_ref:
                kdec = kdec_ref[c // dk]
                kdt_ref[c:c + half, :] = (r1 * kdec).T.astype(BF16)
                kdt_ref[c + half:c + dk, :] = (r2 * kdec).T.astype(BF16)
    for base, out_ref in ((2 * qk_width, v_ref), (2 * qk_width + v_width, gate_ref)):
        for c in range(0, v_width, 2 * dk):
            out_ref[:, c:c + 2 * dk] = _dot(hb, w_ref[:, base + c:base + c + 2 * dk]).astype(BF16)


def _ret_proj(x2, g, w, cos, sin, *, seq, tm=256):
    T, D = x2.shape
    dk = D // RET_HEADS
    qk_width, v_width = D, 2 * D
    sblocks = seq // tm
    row = lambda i: (i, 0)
    kdec = jnp.tile(_retention_decays()[2], (1, tm // RET_CHUNK, 1))
    return pl.pallas_call(
        functools.partial(_ret_proj_kernel, dk=dk, k_scale=dk ** -0.5),
        grid=(T // tm,),
        in_specs=[
            pl.BlockSpec((tm, D), row),
            pl.BlockSpec((1, D), lambda i: (0, 0)),
            pl.BlockSpec(w.shape, lambda i: (0, 0)),
            pl.BlockSpec((tm, dk // 2), lambda i: (i % sblocks, 0)),
            pl.BlockSpec((tm, dk // 2), lambda i: (i % sblocks, 0)),
            pl.BlockSpec(kdec.shape, lambda i: (0, 0, 0)),
        ],
        out_specs=[
            pl.BlockSpec((tm, qk_width), row),
            pl.BlockSpec((tm, qk_width), row),
            pl.BlockSpec((qk_width, tm), lambda i: (0, i)),
            pl.BlockSpec((tm, v_width), row),
            pl.BlockSpec((tm, v_width), row),
        ],
        out_shape=[
            jax.ShapeDtypeStruct((T, qk_width), BF16),
            jax.ShapeDtypeStruct((T, qk_width), BF16),
            jax.ShapeDtypeStruct((qk_width, T), BF16),
            jax.ShapeDtypeStruct((T, v_width), BF16),
            jax.ShapeDtypeStruct((T, v_width), BF16),
        ],
        compiler_params=_params("parallel"),
        name="ret_proj",
    )(x2, g, w, cos, sin, kdec)


def _retention_kernel(q_ref, k_ref, kdt_ref, v_ref, dmat_ref, qdec_ref, cdec_ref, o_ref, state_ref, *, chunk):
    @pl.when(pl.program_id(2) == 0)
    def _():
        state_ref[...] = jnp.zeros_like(state_ref)

    dmat = dmat_ref[0]
    qdec = qdec_ref[0]
    cdec = cdec_ref[0]
    n_chunks = q_ref.shape[0] // chunk
    chunks = [slice(c * chunk, (c + 1) * chunk) for c in range(n_chunks)]
    updates = [_dot(kdt_ref[:, rows], v_ref[rows, :]) for rows in chunks]
    states = [state_ref[...]]
    for upd in updates:
        states.append(states[-1] * cdec + upd)
    state_ref[...] = states[-1]
    for rows, state in zip(chunks, states):
        qc = q_ref[rows, :]
        vc = v_ref[rows, :]
        scores = lax.dot_general(qc, k_ref[rows, :], NT_DIMS, preferred_element_type=F32) * dmat
        o_ref[rows, :] = _dot(scores.astype(BF16), vc) + _dot(qc, state.astype(BF16)) * qdec


def _retention_decays():
    H, C = RET_HEADS, RET_CHUNK
    log_g = jnp.log(1.0 - 2.0 ** (-5.0 - jnp.arange(H, dtype=F32)))
    j = jnp.arange(C, dtype=F32)
    rel = j[:, None] - j[None, :]
    dmat = jnp.where(rel[None] >= 0, jnp.exp(rel[None] * log_g[:, None, None]), 0.0)
    qdec = jnp.exp((j[None, :] + 1.0) * log_g[:, None])[:, :, None]
    kdec = jnp.exp((C - 1.0 - j[None, :]) * log_g[:, None])[:, :, None]
    cdec = jnp.exp(C * log_g)[:, None, None]
    return dmat, qdec, kdec, cdec


def _retention(q, k, kdt, v, *, batch, seq, rows_per_step=2048):
    T = q.shape[0]
    rows_per_step = min(rows_per_step, seq)
    H, C = RET_HEADS, RET_CHUNK
    dk = q.shape[1] // H
    dv = v.shape[1] // H
    nsb = seq // rows_per_step
    dmat, qdec, _, cdec = _retention_decays()

    tok = lambda b, h, s: (b * nsb + s, h)
    per_head = lambda b, h, s: (h, 0, 0)
    return pl.pallas_call(
        functools.partial(_retention_kernel, chunk=C),
        grid=(batch, H, nsb),
        in_specs=[
            pl.BlockSpec((rows_per_step, dk), tok),
            pl.BlockSpec((rows_per_step, dk), tok),
            pl.BlockSpec((dk, rows_per_step), lambda b, h, s: (h, b * nsb + s)),
            pl.BlockSpec((rows_per_step, dv), tok),
            pl.BlockSpec((1, C, C), per_head),
            pl.BlockSpec((1, C, 1), per_head),
            pl.BlockSpec((1, 1, 1), per_head),
        ],
        out_specs=pl.BlockSpec((rows_per_step, dv), tok),
        out_shape=jax.ShapeDtypeStruct((T, v.shape[1]), F32),
        scratch_shapes=[pltpu.VMEM((dk, dv), F32)],
        compiler_params=_params("parallel", "parallel", "arbitrary"),
        name="retention",
    )(q, k, kdt, v, dmat, qdec, cdec)


def _ffn_tail(x1, g_ref, win_ref, wout_ref, fg_ref, out_ref, *, chunk, final_norm):
    d_ff = wout_ref.shape[0]
    hb = (_rms_scale(x1) * g_ref[...]).astype(BF16)
    out_ref[...] = x1
    for c in range(0, d_ff, chunk):
        gate = _dot(hb, win_ref[:, c:c + chunk])
        up = _dot(hb, win_ref[:, d_ff + c:d_ff + c + chunk])
        act = (_silu(gate) * up).astype(BF16)
        out_ref[...] += _dot(act, wout_ref[c:c + chunk, :])
    if final_norm:
        out_ref[...] = _rms_scale(out_ref[...]) * fg_ref[...]


def _ret_out_ffn_kernel(x_ref, o_ref, gate_ref, gn_ref, wo_ref, g_ref, win_ref, wout_ref, fg_ref, out_ref,
                        *, dv, chunk, final_norm):
    x1 = x_ref[...]
    for c in range(0, o_ref.shape[1], dv):
        o = o_ref[:, c:c + dv]
        oc = o - jnp.mean(o, axis=-1, keepdims=True)
        on = oc * lax.rsqrt(jnp.mean(oc * oc, axis=-1, keepdims=True) + GN_EPS) * gn_ref[:, c:c + dv]
        y = (_silu(gate_ref[:, c:c + dv].astype(F32)) * on).astype(BF16)
        x1 = x1 + _dot(y, wo_ref[c:c + dv, :])
    _ffn_tail(x1, g_ref, win_ref, wout_ref, fg_ref, out_ref, chunk=chunk, final_norm=final_norm)


def _sb_out_ffn_kernel(x_ref, o_ref, wo_ref, g_ref, win_ref, wout_ref, fg_ref, out_ref, *, chunk, final_norm):
    heads = jnp.concatenate([o_ref[0, h] for h in range(o_ref.shape[1])], axis=1)
    x1 = x_ref[...] + _dot(heads, wo_ref[...])
    _ffn_tail(x1, g_ref, win_ref, wout_ref, fg_ref, out_ref, chunk=chunk, final_norm=final_norm)


def _resident(shape):
    return pl.BlockSpec(shape, lambda i: (0,) * len(shape), pipeline_mode=pl.Buffered(1))


def _mixer_out_ffn(body, x2, mixer_specs, mixer_args, g, w_in, w_out, final_g, *, name, tm):
    T, D = x2.shape
    row = lambda i: (i, 0)
    return pl.pallas_call(
        body,
        grid=(T // tm,),
        in_specs=[pl.BlockSpec((tm, D), row)] + mixer_specs + [
            _resident((1, D)), _resident(w_in.shape), _resident(w_out.shape), _resident((1, D))],
        out_specs=pl.BlockSpec((tm, D), row),
        out_shape=jax.ShapeDtypeStruct((T, D), F32),
        compiler_params=_params("parallel"),
        name=name,
    )(x2, *mixer_args, g, w_in, w_out, final_g)


def _ret_out_ffn(x2, o, gate, gn_g, wo, g, w_in, w_out, final_g, *, final_norm, tm=512, chunk=256):
    width = o.shape[1]
    row = lambda i: (i, 0)
    body = functools.partial(_ret_out_ffn_kernel, dv=width // RET_HEADS, chunk=chunk, final_norm=final_norm)
    specs = [pl.BlockSpec((tm, width), row), pl.BlockSpec((tm, width), row), _resident((1, width)),
             _resident(wo.shape)]
    return _mixer_out_ffn(body, x2, specs, (o, gate, gn_g, wo), g, w_in, w_out, final_g,
                          name="ret_out_ffn", tm=tm)


def _sb_out_ffn(x2, o, wo, g, w_in, w_out, final_g, *, seq, final_norm, tm=512, chunk=256):
    _, H, _, dh = o.shape
    sblocks = seq // tm
    body = functools.partial(_sb_out_ffn_kernel, chunk=chunk, final_norm=final_norm)
    specs = [pl.BlockSpec((1, H, tm, dh), lambda i: (i // sblocks, 0, i % sblocks, 0)), _resident(wo.shape)]
    return _mixer_out_ffn(body, x2, specs, (o, wo), g, w_in, w_out, final_g, name="sb_out_ffn", tm=tm)


def _sb_proj_kernel(x_ref, gkv_ref, gq_ref, wkv_ref, wq_ref, q_ref, k_ref, vt_ref, *, dh, q_scale, group):
    r = _rms_scale(x_ref[...])
    hkv = (r * gkv_ref[...]).astype(BF16)
    hq = (r * gq_ref[...]).astype(BF16)
    D = wq_ref.shape[1]
    width = group * dh
    for c in range(0, D, width):
        tq = _dot(hq, wq_ref[:, c:c + width]) * q_scale
        tk = _dot(hkv, wkv_ref[:, c:c + width])
        tv = _dot(hkv, wkv_ref[:, D + c:D + c + width])
        for g in range(group):
            h = c // dh + g
            lanes = slice(g * dh, (g + 1) * dh)
            q_ref[0, h] = tq[:, lanes].astype(BF16)
            k_ref[0, h] = tk[:, lanes].astype(BF16)
            vt_ref[0, h] = tv[:, lanes].T.astype(BF16)


def _sb_proj(x2, gkv, gq, wkv, wq, *, batch, seq, tm=512):
    T, D = x2.shape
    H = SB_HEADS
    dh = D // H
    sblocks = seq // tm
    const = lambda i: (0, 0)
    head_major = pl.BlockSpec((1, H, tm, dh), lambda i: (i // sblocks, 0, i % sblocks, 0))
    shape = jax.ShapeDtypeStruct((batch, H, seq, dh), BF16)
    return pl.pallas_call(
        functools.partial(_sb_proj_kernel, dh=dh, q_scale=dh ** -0.5 * LOG2_E, group=4),
        grid=(T // tm,),
        in_specs=[
            pl.BlockSpec((tm, D), lambda i: (i, 0)),
            pl.BlockSpec((1, D), const),
            pl.BlockSpec((1, D), const),
            pl.BlockSpec(wkv.shape, const),
            pl.BlockSpec(wq.shape, const),
        ],
        out_specs=[head_major, head_major,
                   pl.BlockSpec((1, H, dh, tm), lambda i: (i // sblocks, 0, 0, i % sblocks))],
        out_shape=[shape, shape, jax.ShapeDtypeStruct((batch, H, dh, seq), BF16)],
        compiler_params=_params("parallel"),
        name="sb_proj",
    )(x2, gkv, gq, wkv, wq)


MASKED_LOG2 = -1e30
SB_PIPELINE_DEPTH = 4
SB_TABLE_FIELDS = 3


def _sb_streams(seq, tq, tk):
    band = tq // tk
    streams = [[], []]
    for qt in range(seq // tq):
        q0 = qt * tq
        for n in range(band + band * qt):
            streams[0 if n < band else 1].append((q0, q0 + (band - 1 - n) * tk, q0))
    counts = [len(s) for s in streams]
    flat, bases = [], []
    for s in streams:
        s = s + [(0, 0, seq)] * SB_PIPELINE_DEPTH
        bases.append(len(flat))
        for field in range(SB_TABLE_FIELDS):
            flat.extend(e[field] for e in s)
    return counts, bases, jnp.asarray(flat, jnp.int32)


def _sb_attn_kernel(tab_ref, q_ref, k_ref, vt_ref, tri_ref, o_ref, z_ref, sp_ref, tot_ref, m_ref, a_ref,
                    acct_ref, ls_ref, *, tq, tk, counts, bases, unroll):
    seq = q_ref.shape[2]
    lanes = tot_ref.shape[2]
    band = tq // tk
    depth = SB_PIPELINE_DEPTH
    assert band in (1, 2)
    sign_bit = jnp.uint32(0x80000000)

    def run_phase(p, slot, base, length, masked, last_stage=4):
        other = 1 - slot

        def field(f, stage):
            return tab_ref[base + f * length + (p - stage)]

        if last_stage >= 4:
            key0 = pl.multiple_of(field(1, 4), tk)
            rows = pl.ds(pl.multiple_of(field(2, 4), tq), tq)
            acct_ref[:, rows] += lax.dot_general(vt_ref[0, 0, :, pl.ds(key0, tk)], a_ref[other], NT_DIMS,
                                                 preferred_element_type=F32)
        if last_stage >= 3:
            rows = pl.ds(pl.multiple_of(field(2, 3), tq), tq)
            ls = ls_ref[rows, :]
            ls_ref[rows, :] = ls + tot_ref[slot]
            a = jnp.exp2(m_ref[other] - jnp.concatenate([ls] * (tk // lanes), axis=1))
            a_ref[slot] = a.astype(BF16)
        if last_stage >= 2:
            m_ref[slot] = z_ref[slot] - _dot(sp_ref[other], tri_ref[...])
        if last_stage >= 1:
            z = z_ref[other]
            neg_abs = pltpu.bitcast(pltpu.bitcast(z, jnp.uint32) | sign_bit, F32)
            sp = jnp.maximum(z, 0.0) + jnp.log2(1.0 + jnp.exp2(neg_abs))
            if masked:
                diag_shift = (band - 1 - (slot - 1) % band) * tk
                valid = (lax.broadcasted_iota(jnp.int32, (tq, tk), 1) + diag_shift
                         < lax.broadcasted_iota(jnp.int32, (tq, tk), 0))
                sp = jnp.where(valid, sp, 0.0)
                z_ref[other] = jnp.where(valid, z, MASKED_LOG2)
            sp_ref[slot] = sp.astype(BF16)
            tot_ref[slot] = jnp.broadcast_to(jnp.sum(sp, axis=-1, keepdims=True), (tq, lanes))
        qrow = pl.multiple_of(field(0, 0), tq)
        key0 = pl.multiple_of(field(1, 0), tk)
        z_ref[slot] = lax.dot_general(q_ref[0, 0, pl.ds(qrow, tq), :], k_ref[0, 0, pl.ds(key0, tk), :],
                                      NT_DIMS, preferred_element_type=F32)

    def run_stream(index, masked, unroll):
        count, base = counts[index], bases[index]
        length = count + depth
        unroll = max(u for u in range(2, unroll + 1, 2) if count % u == 0)
        assert depth % 2 == 0
        for p in range(depth):
            run_phase(p, p % 2, base, length, masked, last_stage=p)

        def steady(i, c):
            for j in range(unroll):
                run_phase(depth + unroll * i + j, j % 2, base, length, masked)
            return c

        lax.fori_loop(0, count // unroll, steady, 0)

    acct_ref[...] = jnp.zeros_like(acct_ref)
    ls_ref[...] = jnp.zeros_like(ls_ref)
    run_stream(0, True, min(unroll, 8))
    run_stream(1, False, unroll)

    def write_out(i, c):
        rows = pl.ds(pl.multiple_of(i * tq, tq), tq)
        o_ref[0, 0, rows, :] = acct_ref[:, rows].T.astype(o_ref.dtype)
        return c

    lax.fori_loop(0, seq // tq, write_out, 0)


def _sb_attn(q, k, vt, *, tq=256, tk=256, lanes=128, unroll=62):
    B, H, S, dh = q.shape
    idx = jnp.arange(tk)
    tri = (idx[:, None] >= idx[None, :]).astype(BF16)
    counts, bases, table = _sb_streams(S, tq, tk)
    per_head = pl.BlockSpec((1, 1, S, dh), lambda b, h, tab: (b, h, 0, 0))
    return pl.pallas_call(
        functools.partial(_sb_attn_kernel, tq=tq, tk=tk, counts=counts, bases=bases, unroll=unroll),
        grid_spec=pltpu.PrefetchScalarGridSpec(
            num_scalar_prefetch=1,
            grid=(B, H),
            in_specs=[per_head, per_head, pl.BlockSpec((1, 1, dh, S), lambda b, h, tab: (b, h, 0, 0)),
                      pl.BlockSpec(tri.shape, lambda b, h, tab: (0, 0))],
            out_specs=per_head,
            scratch_shapes=[
                pltpu.VMEM((2, tq, tk), F32),
                pltpu.VMEM((2, tq, tk), BF16),
                pltpu.VMEM((2, tq, lanes), F32),
                pltpu.VMEM((2, tq, tk), F32),
                pltpu.VMEM((2, tq, tk), BF16),
                pltpu.VMEM((dh, S + tq), F32),
                pltpu.VMEM((S + tq, lanes), F32),
            ],
        ),
        out_shape=jax.ShapeDtypeStruct((B, H, S, dh), BF16),
        compiler_params=_params("parallel", "parallel"),
        name="sb_attn",
    )(table, q, k, vt, tri)


def _rotary_tables(seq, dk):
    half = dk // 2
    inv_freq = ROPE_BASE ** (-jnp.arange(half, dtype=F32) / half)
    ang = jnp.arange(seq, dtype=F32)[:, None] * inv_freq[None, :]
    return jnp.cos(ang), jnp.sin(ang)


def kernel(x, mix_norm_g, w_ret_in, ret_gn_g, w_ret_out, kv_norm_g, w_kv, w_sb_q, w_sb_out,
           ffn_norm_g, w_ffn_in, w_ffn_out, final_norm_g):
    B, S, D = x.shape
    depth = mix_norm_g.shape[0]
    n_ret = w_ret_in.shape[0]
    assert w_sb_q.shape[0] == 1 and depth == n_ret + 1
    x2 = x.reshape(B * S, D)
    cos, sin = _rotary_tables(S, D // RET_HEADS)
    final_g = final_norm_g.reshape(1, D)

    for layer in range(depth):
        mix_g = mix_norm_g[layer].reshape(1, D)
        ffn_args = (ffn_norm_g[layer].reshape(1, D), w_ffn_in[layer].astype(BF16),
                    w_ffn_out[layer].astype(BF16), final_g)
        final_norm = layer == depth - 1
        if layer < n_ret:
            q, k, kdt, v, gate = _ret_proj(x2, mix_g, w_ret_in[layer].astype(BF16), cos, sin, seq=S)
            o = _retention(q, k, kdt, v, batch=B, seq=S)
            x2 = _ret_out_ffn(x2, o, gate, ret_gn_g[layer].reshape(1, -1), w_ret_out[layer].astype(BF16),
                              *ffn_args, final_norm=final_norm)
        else:
            q, k_sb, vt_sb = _sb_proj(x2, kv_norm_g.reshape(1, D), mix_g,
                                     w_kv.astype(BF16), w_sb_q[0].astype(BF16), batch=B, seq=S)
            o = _sb_attn(q, k_sb, vt_sb)
            x2 = _sb_out_ffn(x2, o, w_sb_out[0].astype(BF16), *ffn_args, seq=S, final_norm=final_norm)
    return x2.reshape(B, S, D)
```

```python
import functools

import jax
import jax.numpy as jnp
from jax import lax
from jax.experimental import pallas as pl
from jax.experimental.pallas import tpu as pltpu

RMS_EPS = 1e-6
GN_EPS = 1e-5
RET_HEADS = 4
RET_CHUNK = 128
ROPE_BASE = 10000.0
SB_HEADS = 16

LOG2_E = 1.4426950408889634

F32 = jnp.float32
BF16 = jnp.bfloat16

V7X_VMEM_BYTES = 64 * 1024 * 1024
VMEM_LIMIT_BYTES = (V7X_VMEM_BYTES * 7) // 8

NT_DIMS = (((1,), (1,)), ((), ()))
TN_DIMS = (((0,), (0,)), ((), ()))


def _params(*semantics):
    return pltpu.CompilerParams(dimension_semantics=semantics, vmem_limit_bytes=VMEM_LIMIT_BYTES)


def _rms_scale(x):
    return x * lax.rsqrt(jnp.mean(x * x, axis=-1, keepdims=True) + RMS_EPS)


def _silu(g):
    return g * (1.0 / (1.0 + jnp.exp(-g)))


def _dot(a, b):
    return jnp.dot(a, b, preferred_element_type=F32)


def _ret_proj_kernel(x_ref, g_ref, w_ref, cos_ref, sin_ref, kdec_ref, q_ref, k_ref, kdt_ref, v_ref, gate_ref,
                     *, dk, k_scale):
    hb = (_rms_scale(x_ref[...]) * g_ref[...]).astype(BF16)
    cos = cos_ref[...]
    sin = sin_ref[...]
    half = dk // 2
    qk_width = q_ref.shape[1]
    v_width = v_ref.shape[1]
    for base, out_ref, scale in ((0, q_ref, 1.0), (qk_width, k_ref, k_scale)):
        for c in range(0, qk_width, dk):
            t = _dot(hb, w_ref[:, base + c:base + c + dk])
            t1, t2 = t[:, :half], t[:, half:]
            r1 = (t1 * cos - t2 * sin) * scale
            r2 = (t1 * sin + t2 * cos) * scale
            out_ref[:, c:c + half] = r1.astype(BF16)
            out_ref[:, c + half:c + dk] = r2.astype(BF16)
            if out_ref is k_ref:
                kdec = kdec_ref[c // dk]
                kdt_ref[c:c + half, :] = (r1 * kdec).T.astype(BF16)
                kdt_ref[c + half:c + dk, :] = (r2 * kdec).T.astype(BF16)
    for base, out_ref in ((2 * qk_width, v_ref), (2 * qk_width + v_width, gate_ref)):
        for c in range(0, v_width, 2 * dk):
            out_ref[:, c:c + 2 * dk] = _dot(hb, w_ref[:, base + c:base + c + 2 * dk]).astype(BF16)


def _ret_proj(x2, g, w, cos, sin, *, seq, tm=512):
    T, D = x2.shape
    dk = D // RET_HEADS
    qk_width, v_width = D, 2 * D
    sblocks = seq // tm
    row = lambda i: (i, 0)
    kdec = jnp.tile(_retention_decays()[2], (1, tm // RET_CHUNK, 1))
    return pl.pallas_call(
        functools.partial(_ret_proj_kernel, dk=dk, k_scale=dk ** -0.5),
        grid=(T // tm,),
        in_specs=[
            pl.BlockSpec((tm, D), row),
            _resident((1, D)),
            _resident(w.shape),
            pl.BlockSpec((tm, dk // 2), lambda i: (i % sblocks, 0)),
            pl.BlockSpec((tm, dk // 2), lambda i: (i % sblocks, 0)),
            _resident(kdec.shape),
        ],
        out_specs=[
            pl.BlockSpec((tm, qk_width), row),
            pl.BlockSpec((tm, qk_width), row),
            pl.BlockSpec((qk_width, tm), lambda i: (0, i)),
            pl.BlockSpec((tm, v_width), row),
            pl.BlockSpec((tm, v_width), row),
        ],
        out_shape=[
            jax.ShapeDtypeStruct((T, qk_width), BF16),
            jax.ShapeDtypeStruct((T, qk_width), BF16),
            jax.ShapeDtypeStruct((qk_width, T), BF16),
            jax.ShapeDtypeStruct((T, v_width), BF16),
            jax.ShapeDtypeStruct((T, v_width), BF16),
        ],
        compiler_params=_params("parallel"),
        name="ret_proj",
    )(x2, g, w, cos, sin, kdec)


def _retention_kernel(q_ref, k_ref, kdt_ref, v_ref, dmat_ref, qdec_ref, cdec_ref, o_ref, state_ref, *, chunk):
    @pl.when(pl.program_id(2) == 0)
    def _():
        state_ref[...] = jnp.zeros_like(state_ref)

    dmat = dmat_ref[0]
    qdec = qdec_ref[0]
    cdec = cdec_ref[0]
    n_chunks = q_ref.shape[0] // chunk
    chunks = [slice(c * chunk, (c + 1) * chunk) for c in range(n_chunks)]
    updates = [_dot(kdt_ref[:, rows], v_ref[rows, :]) for rows in chunks]
    states = [state_ref[...]]
    for upd in updates:
        states.append(states[-1] * cdec + upd)
    state_ref[...] = states[-1]
    for rows, state in zip(chunks, states):
        qc = q_ref[rows, :]
        vc = v_ref[rows, :]
        scores = lax.dot_general(qc, k_ref[rows, :], NT_DIMS, preferred_element_type=F32) * dmat
        o_ref[rows, :] = _dot(scores.astype(BF16), vc) + _dot(qc, state.astype(BF16)) * qdec


def _retention_decays():
    H, C = RET_HEADS, RET_CHUNK
    log_g = jnp.log(1.0 - 2.0 ** (-5.0 - jnp.arange(H, dtype=F32)))
    j = jnp.arange(C, dtype=F32)
    rel = j[:, None] - j[None, :]
    dmat = jnp.where(rel[None] >= 0, jnp.exp(rel[None] * log_g[:, None, None]), 0.0)
    qdec = jnp.exp((j[None, :] + 1.0) * log_g[:, None])[:, :, None]
    kdec = jnp.exp((C - 1.0 - j[None, :]) * log_g[:, None])[:, :, None]
    cdec = jnp.exp(C * log_g)[:, None, None]
    return dmat, qdec, kdec, cdec


def _retention(q, k, kdt, v, *, batch, seq, rows_per_step=2048):
    T = q.shape[0]
    rows_per_step = min(rows_per_step, seq)
    H, C = RET_HEADS, RET_CHUNK
    dk = q.shape[1] // H
    dv = v.shape[1] // H
    nsb = seq // rows_per_step
    dmat, qdec, _, cdec = _retention_decays()

    tok = lambda b, h, s: (b * nsb + s, h)
    per_head = lambda b, h, s: (h, 0, 0)
    return pl.pallas_call(
        functools.partial(_retention_kernel, chunk=C),
        grid=(batch, H, nsb),
        in_specs=[
            pl.BlockSpec((rows_per_step, dk), tok),
            pl.BlockSpec((rows_per_step, dk), tok),
            pl.BlockSpec((dk, rows_per_step), lambda b, h, s: (h, b * nsb + s)),
            pl.BlockSpec((rows_per_step, dv), tok),
            pl.BlockSpec((1, C, C), per_head),
            pl.BlockSpec((1, C, 1), per_head),
            pl.BlockSpec((1, 1, 1), per_head),
        ],
        out_specs=pl.BlockSpec((rows_per_step, dv), tok),
        out_shape=jax.ShapeDtypeStruct((T, v.shape[1]), F32),
        scratch_shapes=[pltpu.VMEM((dk, dv), F32)],
        compiler_params=_params("parallel", "parallel", "arbitrary"),
        name="retention",
    )(q, k, kdt, v, dmat, qdec, cdec)


def _ffn_tail(x1, g_ref, win_ref, wout_ref, fg_ref, out_ref, *, chunk, final_norm):
    d_ff = wout_ref.shape[0]
    hb = (_rms_scale(x1) * g_ref[...]).astype(BF16)
    out_ref[...] = x1
    for c in range(0, d_ff, chunk):
        gate = _dot(hb, win_ref[:, c:c + chunk])
        up = _dot(hb, win_ref[:, d_ff + c:d_ff + c + chunk])
        act = (_silu(gate) * up).astype(BF16)
        out_ref[...] += _dot(act, wout_ref[c:c + chunk, :])
    if final_norm:
        out_ref[...] = _rms_scale(out_ref[...]) * fg_ref[...]


def _ret_out_ffn_kernel(x_ref, o_ref, gate_ref, gn_ref, wo_ref, g_ref, win_ref, wout_ref, fg_ref, out_ref,
                        *, dv, chunk, final_norm):
    x1 = x_ref[...]
    for c in range(0, o_ref.shape[1], dv):
        o = o_ref[:, c:c + dv]
        oc = o - jnp.mean(o, axis=-1, keepdims=True)
        on = oc * lax.rsqrt(jnp.mean(oc * oc, axis=-1, keepdims=True) + GN_EPS) * gn_ref[:, c:c + dv]
        y = (_silu(gate_ref[:, c:c + dv].astype(F32)) * on).astype(BF16)
        x1 = x1 + _dot(y, wo_ref[c:c + dv, :])
    _ffn_tail(x1, g_ref, win_ref, wout_ref, fg_ref, out_ref, chunk=chunk, final_norm=final_norm)


def _sb_out_ffn_kernel(x_ref, ot_ref, wo_ref, g_ref, win_ref, wout_ref, fg_ref, out_ref, *, chunk, final_norm):
    H, dh, tm = ot_ref.shape[1:]
    heads_t = ot_ref[0].reshape(H * dh, tm)
    x1 = x_ref[...] + lax.dot_general(heads_t, wo_ref[...], TN_DIMS, preferred_element_type=F32)
    _ffn_tail(x1, g_ref, win_ref, wout_ref, fg_ref, out_ref, chunk=chunk, final_norm=final_norm)


def _resident(shape):
    return pl.BlockSpec(shape, lambda i: (0,) * len(shape), pipeline_mode=pl.Buffered(1))


def _mixer_out_ffn(body, x2, mixer_specs, mixer_args, g, w_in, w_out, final_g, *, name, tm):
    T, D = x2.shape
    row = lambda i: (i, 0)
    return pl.pallas_call(
        body,
        grid=(T // tm,),
        in_specs=[pl.BlockSpec((tm, D), row)] + mixer_specs + [
            _resident((1, D)), _resident(w_in.shape), _resident(w_out.shape), _resident((1, D))],
        out_specs=pl.BlockSpec((tm, D), row),
        out_shape=jax.ShapeDtypeStruct((T, D), F32),
        compiler_params=_params("parallel"),
        name=name,
    )(x2, *mixer_args, g, w_in, w_out, final_g)


def _ret_out_ffn(x2, o, gate, gn_g, wo, g, w_in, w_out, final_g, *, final_norm, tm=512, chunk=256):
    width = o.shape[1]
    row = lambda i: (i, 0)
    body = functools.partial(_ret_out_ffn_kernel, dv=width // RET_HEADS, chunk=chunk, final_norm=final_norm)
    specs = [pl.BlockSpec((tm, width), row), pl.BlockSpec((tm, width), row), _resident((1, width)),
             _resident(wo.shape)]
    return _mixer_out_ffn(body, x2, specs, (o, gate, gn_g, wo), g, w_in, w_out, final_g,
                          name="ret_out_ffn", tm=tm)


def _sb_out_ffn(x2, o, wo, g, w_in, w_out, final_g, *, seq, final_norm, tm=512, chunk=256):
    _, H, dh, _ = o.shape
    sblocks = seq // tm
    body = functools.partial(_sb_out_ffn_kernel, chunk=chunk, final_norm=final_norm)
    specs = [pl.BlockSpec((1, H, dh, tm), lambda i: (i // sblocks, 0, 0, i % sblocks)), _resident(wo.shape)]
    return _mixer_out_ffn(body, x2, specs, (o, wo), g, w_in, w_out, final_g, name="sb_out_ffn", tm=tm)


def _sb_proj_kernel(x_ref, gkv_ref, gq_ref, wkv_ref, wq_ref, q_ref, k_ref, vt_ref, *, dh, q_scale, group):
    r = _rms_scale(x_ref[...])
    hkv = (r * gkv_ref[...]).astype(BF16)
    hq = (r * gq_ref[...]).astype(BF16)
    D = wq_ref.shape[1]
    width = group * dh
    for c in range(0, D, width):
        tq = _dot(hq, wq_ref[:, c:c + width]) * q_scale
        tk = _dot(hkv, wkv_ref[:, c:c + width])
        tv = _dot(hkv, wkv_ref[:, D + c:D + c + width])
        for g in range(group):
            h = c // dh + g
            lanes = slice(g * dh, (g + 1) * dh)
            q_ref[0, h] = tq[:, lanes].astype(BF16)
            k_ref[0, h] = tk[:, lanes].astype(BF16)
            vt_ref[0, h] = tv[:, lanes].T.astype(BF16)


def _sb_proj(x2, gkv, gq, wkv, wq, *, batch, seq, tm=512):
    T, D = x2.shape
    H = SB_HEADS
    dh = D // H
    sblocks = seq // tm
    const = lambda i: (0, 0)
    head_major = pl.BlockSpec((1, H, tm, dh), lambda i: (i // sblocks, 0, i % sblocks, 0))
    shape = jax.ShapeDtypeStruct((batch, H, seq, dh), BF16)
    return pl.pallas_call(
        functools.partial(_sb_proj_kernel, dh=dh, q_scale=dh ** -0.5 * LOG2_E, group=4),
        grid=(T // tm,),
        in_specs=[
            pl.BlockSpec((tm, D), lambda i: (i, 0)),
            pl.BlockSpec((1, D), const),
            pl.BlockSpec((1, D), const),
            pl.BlockSpec(wkv.shape, const),
            pl.BlockSpec(wq.shape, const),
        ],
        out_specs=[head_major, head_major,
                   pl.BlockSpec((1, H, dh, tm), lambda i: (i // sblocks, 0, 0, i % sblocks))],
        out_shape=[shape, shape, jax.ShapeDtypeStruct((batch, H, dh, seq), BF16)],
        compiler_params=_params("parallel"),
        name="sb_proj",
    )(x2, gkv, gq, wkv, wq)


MASKED_LOG2 = -1e30
SB_PIPELINE_DEPTH = 4
SB_TABLE_FIELDS = 3


def _sb_streams(seq, tq, tk):
    band = tq // tk
    streams = [[], []]
    for qt in range(seq // tq):
        q0 = qt * tq
        for n in range(band + band * qt):
            streams[0 if n < band else 1].append((q0, q0 + (band - 1 - n) * tk, q0))
    counts = [len(s) for s in streams]
    flat, bases = [], []
    for s in streams:
        s = s + [(0, 0, seq)] * SB_PIPELINE_DEPTH
        bases.append(len(flat))
        for field in range(SB_TABLE_FIELDS):
            flat.extend(e[field] for e in s)
    return counts, bases, jnp.asarray(flat, jnp.int32)


def _sb_attn_kernel(tab_ref, q_ref, k_ref, vt_ref, tri_ref, o_ref, z_ref, sp_ref, tot_ref, m_ref, a_ref,
                    acct_ref, ls_ref, *, tq, tk, counts, bases, unroll):
    seq = q_ref.shape[2]
    lanes = tot_ref.shape[2]
    band = tq // tk
    depth = SB_PIPELINE_DEPTH
    assert band in (1, 2)
    sign_bit = jnp.uint32(0x80000000)

    def run_phase(p, slot, base, length, masked, last_stage=4):
        other = 1 - slot

        def field(f, stage):
            return tab_ref[base + f * length + (p - stage)]

        if last_stage >= 4:
            key0 = pl.multiple_of(field(1, 4), tk)
            rows = pl.ds(pl.multiple_of(field(2, 4), tq), tq)
            acct_ref[:, rows] += lax.dot_general(vt_ref[0, 0, :, pl.ds(key0, tk)], a_ref[other], NT_DIMS,
                                                 preferred_element_type=F32)
        if last_stage >= 3:
            rows = pl.ds(pl.multiple_of(field(2, 3), tq), tq)
            ls = ls_ref[rows, :]
            ls_ref[rows, :] = ls + tot_ref[slot]
            a = jnp.exp2(m_ref[other] - jnp.concatenate([ls] * (tk // lanes), axis=1))
            a_ref[slot] = a.astype(BF16)
        if last_stage >= 2:
            m_ref[slot] = z_ref[slot] - _dot(sp_ref[other], tri_ref[...])
        if last_stage >= 1:
            z = z_ref[other]
            neg_abs = pltpu.bitcast(pltpu.bitcast(z, jnp.uint32) | sign_bit, F32)
            sp = jnp.maximum(z, 0.0) + jnp.log2(1.0 + jnp.exp2(neg_abs))
            if masked:
                diag_shift = (band - 1 - (slot - 1) % band) * tk
                valid = (lax.broadcasted_iota(jnp.int32, (tq, tk), 1) + diag_shift
                         < lax.broadcasted_iota(jnp.int32, (tq, tk), 0))
                sp = jnp.where(valid, sp, 0.0)
                z_ref[other] = jnp.where(valid, z, MASKED_LOG2)
            sp_ref[slot] = sp.astype(BF16)
            tot_ref[slot] = jnp.broadcast_to(jnp.sum(sp, axis=-1, keepdims=True), (tq, lanes))
        qrow = pl.multiple_of(field(0, 0), tq)
        key0 = pl.multiple_of(field(1, 0), tk)
        z_ref[slot] = lax.dot_general(q_ref[0, 0, pl.ds(qrow, tq), :], k_ref[0, 0, pl.ds(key0, tk), :],
                                      NT_DIMS, preferred_element_type=F32)

    def run_stream(index, masked, unroll):
        count, base = counts[index], bases[index]
        length = count + depth
        unroll = max(u for u in range(2, unroll + 1, 2) if count % u == 0)
        assert depth % 2 == 0
        for p in range(depth):
            run_phase(p, p % 2, base, length, masked, last_stage=p)

        def steady(i, c):
            for j in range(unroll):
                run_phase(depth + unroll * i + j, j % 2, base, length, masked)
            return c

        lax.fori_loop(0, count // unroll, steady, 0)

    acct_ref[...] = jnp.zeros_like(acct_ref)
    ls_ref[...] = jnp.zeros_like(ls_ref)
    run_stream(0, True, min(unroll, 8))
    run_stream(1, False, unroll)

    o_ref[0, 0] = acct_ref[:, :seq].astype(o_ref.dtype)


def _sb_attn(q, k, vt, *, tq=256, tk=256, lanes=128, unroll=62):
    B, H, S, dh = q.shape
    idx = jnp.arange(tk)
    tri = (idx[:, None] >= idx[None, :]).astype(BF16)
    counts, bases, table = _sb_streams(S, tq, tk)
    per_head = pl.BlockSpec((1, 1, S, dh), lambda b, h, tab: (b, h, 0, 0))
    return pl.pallas_call(
        functools.partial(_sb_attn_kernel, tq=tq, tk=tk, counts=counts, bases=bases, unroll=unroll),
        grid_spec=pltpu.PrefetchScalarGridSpec(
            num_scalar_prefetch=1,
            grid=(B, H),
            in_specs=[per_head, per_head, pl.BlockSpec((1, 1, dh, S), lambda b, h, tab: (b, h, 0, 0)),
                      pl.BlockSpec(tri.shape, lambda b, h, tab: (0, 0))],
            out_specs=pl.BlockSpec((1, 1, dh, S), lambda b, h, tab: (b, h, 0, 0)),
            scratch_shapes=[
                pltpu.VMEM((2, tq, tk), F32),
                pltpu.VMEM((2, tq, tk), BF16),
                pltpu.VMEM((2, tq, lanes), F32),
                pltpu.VMEM((2, tq, tk), F32),
                pltpu.VMEM((2, tq, tk), BF16),
                pltpu.VMEM((dh, S + tq), F32),
                pltpu.VMEM((S + tq, lanes), F32),
            ],
        ),
        out_shape=jax.ShapeDtypeStruct((B, H, dh, S), BF16),
        compiler_params=_params("parallel", "parallel"),
        name="sb_attn",
    )(table, q, k, vt, tri)


def _rotary_tables(seq, dk):
    half = dk // 2
    inv_freq = ROPE_BASE ** (-jnp.arange(half, dtype=F32) / half)
    ang = jnp.arange(seq, dtype=F32)[:, None] * inv_freq[None, :]
    return jnp.cos(ang), jnp.sin(ang)


def kernel(x, mix_norm_g, w_ret_in, ret_gn_g, w_ret_out, kv_norm_g, w_kv, w_sb_q, w_sb_out,
           ffn_norm_g, w_ffn_in, w_ffn_out, final_norm_g):
    B, S, D = x.shape
    depth = mix_norm_g.shape[0]
    n_ret = w_ret_in.shape[0]
    assert w_sb_q.shape[0] == 1 and depth == n_ret + 1
    x2 = x.reshape(B * S, D)
    cos, sin = _rotary_tables(S, D // RET_HEADS)
    final_g = final_norm_g.reshape(1, D)

    for layer in range(depth):
        mix_g = mix_norm_g[layer].reshape(1, D)
        ffn_args = (ffn_norm_g[layer].reshape(1, D), w_ffn_in[layer].astype(BF16),
                    w_ffn_out[layer].astype(BF16), final_g)
        final_norm = layer == depth - 1
        if layer < n_ret:
            q, k, kdt, v, gate = _ret_proj(x2, mix_g, w_ret_in[layer].astype(BF16), cos, sin, seq=S)
            o = _retention(q, k, kdt, v, batch=B, seq=S)
            x2 = _ret_out_ffn(x2, o, gate, ret_gn_g[layer].reshape(1, -1), w_ret_out[layer].astype(BF16),
                              *ffn_args, final_norm=final_norm)
        else:
            q, k_sb, vt_sb = _sb_proj(x2, kv_norm_g.reshape(1, D), mix_g,
                                     w_kv.astype(BF16), w_sb_q[0].astype(BF16), batch=B, seq=S)
            o = _sb_attn(q, k_sb, vt_sb)
            x2 = _sb_out_ffn(x2, o, w_sb_out[0].astype(BF16), *ffn_args, seq=S, final_norm=final_norm)
    return x2.reshape(B, S, D)
```

```python
import functools

import jax
import jax.numpy as jnp
from jax import lax
from jax.experimental import pallas as pl
from jax.experimental.pallas import tpu as pltpu

RMS_EPS = 1e-6
GN_EPS = 1e-5
RET_HEADS = 4
RET_CHUNK = 128
ROPE_BASE = 10000.0
SB_HEADS = 16

LOG2_E = 1.4426950408889634

F32 = jnp.float32
BF16 = jnp.bfloat16

V7X_VMEM_BYTES = 64 * 1024 * 1024
VMEM_LIMIT_BYTES = (V7X_VMEM_BYTES * 7) // 8
V7X_LANES = 128

NT_DIMS = (((1,), (1,)), ((), ()))
TN_DIMS = (((0,), (0,)), ((), ()))


def _params(*semantics):
    return pltpu.CompilerParams(dimension_semantics=semantics, vmem_limit_bytes=VMEM_LIMIT_BYTES)


def _rms_scale(x):
    return x * lax.rsqrt(jnp.mean(x * x, axis=-1, keepdims=True) + RMS_EPS)


def _silu(g):
    return g * (1.0 / (1.0 + jnp.exp(-g)))


def _dot(a, b):
    return jnp.dot(a, b, preferred_element_type=F32)


def _ret_proj_kernel(x_ref, g_ref, w_ref, cos_ref, sin_ref, kdec_ref, q_ref, k_ref, kdt_ref, v_ref, gate_ref,
                     *, dk, k_scale):
    hb = (_rms_scale(x_ref[...]) * g_ref[...]).astype(BF16)
    cos = cos_ref[...]
    sin = sin_ref[...]
    half = dk // 2
    qk_width = q_ref.shape[1]
    v_width = v_ref.shape[1]
    for base, out_ref, scale in ((0, q_ref, 1.0), (qk_width, k_ref, k_scale)):
        for c in range(0, qk_width, dk):
            t = _dot(hb, w_ref[:, base + c:base + c + dk])
            t1, t2 = t[:, :half], t[:, half:]
            r1 = (t1 * cos - t2 * sin) * scale
            r2 = (t1 * sin + t2 * cos) * scale
            out_ref[:, c:c + half] = r1.astype(BF16)
            out_ref[:, c + half:c + dk] = r2.astype(BF16)
            if out_ref is k_ref:
                kdec = kdec_ref[c // dk]
                kdt_ref[c:c + half, :] = (r1 * kdec).T.astype(BF16)
                kdt_ref[c + half:c + dk, :] = (r2 * kdec).T.astype(BF16)
    for base, out_ref in ((2 * qk_width, v_ref), (2 * qk_width + v_width, gate_ref)):
        for c in range(0, v_width, 2 * dk):
            out_ref[:, c:c + 2 * dk] = _dot(hb, w_ref[:, base + c:base + c + 2 * dk]).astype(BF16)


def _ret_proj(x2, g, w, cos, sin, *, seq, tm=512):
    T, D = x2.shape
    dk = D // RET_HEADS
    qk_width, v_width = D, 2 * D
    sblocks = seq // tm
    row = lambda i: (i, 0)
    kdec = jnp.tile(_retention_decays()[2], (1, tm // RET_CHUNK, 1))
    return pl.pallas_call(
        functools.partial(_ret_proj_kernel, dk=dk, k_scale=dk ** -0.5),
        grid=(T // tm,),
        in_specs=[
            pl.BlockSpec((tm, D), row),
            _resident((1, D)),
            _resident(w.shape),
            pl.BlockSpec((tm, dk // 2), lambda i: (i % sblocks, 0)),
            pl.BlockSpec((tm, dk // 2), lambda i: (i % sblocks, 0)),
            _resident(kdec.shape),
        ],
        out_specs=[
            pl.BlockSpec((tm, qk_width), row),
            pl.BlockSpec((tm, qk_width), row),
            pl.BlockSpec((qk_width, tm), lambda i: (0, i)),
            pl.BlockSpec((tm, v_width), row),
            pl.BlockSpec((tm, v_width), row),
        ],
        out_shape=[
            jax.ShapeDtypeStruct((T, qk_width), BF16),
            jax.ShapeDtypeStruct((T, qk_width), BF16),
            jax.ShapeDtypeStruct((qk_width, T), BF16),
            jax.ShapeDtypeStruct((T, v_width), BF16),
            jax.ShapeDtypeStruct((T, v_width), BF16),
        ],
        compiler_params=_params("parallel"),
        name="ret_proj",
    )(x2, g, w, cos, sin, kdec)


def _retention_kernel(q_ref, k_ref, kdt_ref, v_ref, dmat_ref, qdec_ref, cdec_ref, o_ref, state_ref, *, chunk):
    @pl.when(pl.program_id(2) == 0)
    def _():
        state_ref[...] = jnp.zeros_like(state_ref)

    dmat = dmat_ref[0]
    qdec = qdec_ref[0]
    cdec = cdec_ref[0]
    n_chunks = q_ref.shape[0] // chunk
    chunks = [slice(c * chunk, (c + 1) * chunk) for c in range(n_chunks)]
    updates = [_dot(kdt_ref[:, rows], v_ref[rows, :]) for rows in chunks]
    states = [state_ref[...]]
    for upd in updates:
        states.append(states[-1] * cdec + upd)
    state_ref[...] = states[-1]
    for rows, state in zip(chunks, states):
        qc = q_ref[rows, :]
        vc = v_ref[rows, :]
        scores = lax.dot_general(qc, k_ref[rows, :], NT_DIMS, preferred_element_type=F32) * dmat
        o_ref[rows, :] = _dot(scores.astype(BF16), vc) + _dot(qc, state.astype(BF16)) * qdec


def _retention_decays():
    H, C = RET_HEADS, RET_CHUNK
    log_g = jnp.log(1.0 - 2.0 ** (-5.0 - jnp.arange(H, dtype=F32)))
    j = jnp.arange(C, dtype=F32)
    rel = j[:, None] - j[None, :]
    dmat = jnp.where(rel[None] >= 0, jnp.exp(rel[None] * log_g[:, None, None]), 0.0)
    qdec = jnp.exp((j[None, :] + 1.0) * log_g[:, None])[:, :, None]
    kdec = jnp.exp((C - 1.0 - j[None, :]) * log_g[:, None])[:, :, None]
    cdec = jnp.exp(C * log_g)[:, None, None]
    return dmat, qdec, kdec, cdec


def _retention(q, k, kdt, v, *, batch, seq, rows_per_step=2048):
    T = q.shape[0]
    rows_per_step = min(rows_per_step, seq)
    H, C = RET_HEADS, RET_CHUNK
    dk = q.shape[1] // H
    dv = v.shape[1] // H
    nsb = seq // rows_per_step
    dmat, qdec, _, cdec = _retention_decays()

    tok = lambda b, h, s: (b * nsb + s, h)
    per_head = lambda b, h, s: (h, 0, 0)
    return pl.pallas_call(
        functools.partial(_retention_kernel, chunk=C),
        grid=(batch, H, nsb),
        in_specs=[
            pl.BlockSpec((rows_per_step, dk), tok),
            pl.BlockSpec((rows_per_step, dk), tok),
            pl.BlockSpec((dk, rows_per_step), lambda b, h, s: (h, b * nsb + s)),
            pl.BlockSpec((rows_per_step, dv), tok),
            pl.BlockSpec((1, C, C), per_head),
            pl.BlockSpec((1, C, 1), per_head),
            pl.BlockSpec((1, 1, 1), per_head),
        ],
        out_specs=pl.BlockSpec((rows_per_step, dv), tok),
        out_shape=jax.ShapeDtypeStruct((T, v.shape[1]), F32),
        scratch_shapes=[pltpu.VMEM((dk, dv), F32)],
        compiler_params=_params("parallel", "parallel", "arbitrary"),
        name="retention",
    )(q, k, kdt, v, dmat, qdec, cdec)


def _ffn_tail(x1, g_ref, win_ref, wout_ref, fg_ref, out_ref, *, chunk, final_norm):
    d_ff = wout_ref.shape[0]
    hb = (_rms_scale(x1) * g_ref[...]).astype(BF16)
    out_ref[...] = x1
    for c in range(0, d_ff, chunk):
        gate = _dot(hb, win_ref[:, c:c + chunk])
        up = _dot(hb, win_ref[:, d_ff + c:d_ff + c + chunk])
        act = (_silu(gate) * up).astype(BF16)
        out_ref[...] += _dot(act, wout_ref[c:c + chunk, :])
    if final_norm:
        out_ref[...] = _rms_scale(out_ref[...]) * fg_ref[...]


def _ret_out_ffn_kernel(x_ref, o_ref, gate_ref, gn_ref, wo_ref, g_ref, win_ref, wout_ref, fg_ref, out_ref,
                        *, dv, chunk, final_norm):
    x1 = x_ref[...]
    for c in range(0, o_ref.shape[1], dv):
        o = o_ref[:, c:c + dv]
        oc = o - jnp.mean(o, axis=-1, keepdims=True)
        on = oc * lax.rsqrt(jnp.mean(oc * oc, axis=-1, keepdims=True) + GN_EPS) * gn_ref[:, c:c + dv]
        y = (_silu(gate_ref[:, c:c + dv].astype(F32)) * on).astype(BF16)
        x1 = x1 + _dot(y, wo_ref[c:c + dv, :])
    _ffn_tail(x1, g_ref, win_ref, wout_ref, fg_ref, out_ref, chunk=chunk, final_norm=final_norm)


def _sb_out_ffn_kernel(x_ref, ot_ref, wo_ref, g_ref, win_ref, wout_ref, fg_ref, out_ref, *, chunk, final_norm):
    H, dh, tm = ot_ref.shape[1:]
    heads_t = ot_ref[0].reshape(H * dh, tm)
    x1 = x_ref[...] + lax.dot_general(heads_t, wo_ref[...], TN_DIMS, preferred_element_type=F32)
    _ffn_tail(x1, g_ref, win_ref, wout_ref, fg_ref, out_ref, chunk=chunk, final_norm=final_norm)


def _resident(shape):
    return pl.BlockSpec(shape, lambda i: (0,) * len(shape), pipeline_mode=pl.Buffered(1))


def _mixer_out_ffn(body, x2, mixer_specs, mixer_args, g, w_in, w_out, final_g, *, name, tm):
    T, D = x2.shape
    row = lambda i: (i, 0)
    return pl.pallas_call(
        body,
        grid=(T // tm,),
        in_specs=[pl.BlockSpec((tm, D), row)] + mixer_specs + [
            _resident((1, D)), _resident(w_in.shape), _resident(w_out.shape), _resident((1, D))],
        out_specs=pl.BlockSpec((tm, D), row),
        out_shape=jax.ShapeDtypeStruct((T, D), F32),
        compiler_params=_params("parallel"),
        name=name,
    )(x2, *mixer_args, g, w_in, w_out, final_g)


def _ret_out_ffn(x2, o, gate, gn_g, wo, g, w_in, w_out, final_g, *, final_norm, tm=512, chunk=256):
    width = o.shape[1]
    row = lambda i: (i, 0)
    body = functools.partial(_ret_out_ffn_kernel, dv=width // RET_HEADS, chunk=chunk, final_norm=final_norm)
    specs = [pl.BlockSpec((tm, width), row), pl.BlockSpec((tm, width), row), _resident((1, width)),
             _resident(wo.shape)]
    return _mixer_out_ffn(body, x2, specs, (o, gate, gn_g, wo), g, w_in, w_out, final_g,
                          name="ret_out_ffn", tm=tm)


def _sb_out_ffn(x2, o, wo, g, w_in, w_out, final_g, *, seq, final_norm, tm=512, chunk=256):
    _, H, dh, _ = o.shape
    sblocks = seq // tm
    body = functools.partial(_sb_out_ffn_kernel, chunk=chunk, final_norm=final_norm)
    specs = [pl.BlockSpec((1, H, dh, tm), lambda i: (i // sblocks, 0, 0, i % sblocks)), _resident(wo.shape)]
    return _mixer_out_ffn(body, x2, specs, (o, wo), g, w_in, w_out, final_g, name="sb_out_ffn", tm=tm)


def _sb_proj_kernel(x_ref, gkv_ref, gq_ref, wkv_ref, wq_ref, q_ref, k_ref, vt_ref, *, dh, q_scale, group):
    r = _rms_scale(x_ref[...])
    hkv = (r * gkv_ref[...]).astype(BF16)
    hq = (r * gq_ref[...]).astype(BF16)
    D = wq_ref.shape[1]
    width = group * dh
    for c in range(0, D, width):
        tq = _dot(hq, wq_ref[:, c:c + width]) * q_scale
        tk = _dot(hkv, wkv_ref[:, c:c + width])
        tv = _dot(hkv, wkv_ref[:, D + c:D + c + width])
        for g in range(group):
            h = c // dh + g
            lanes = slice(g * dh, (g + 1) * dh)
            q_ref[0, h] = tq[:, lanes].astype(BF16)
            k_ref[0, h] = tk[:, lanes].astype(BF16)
            vt_ref[0, h] = tv[:, lanes].T.astype(BF16)


def _sb_proj(x2, gkv, gq, wkv, wq, *, batch, seq, tm=512):
    T, D = x2.shape
    H = SB_HEADS
    dh = D // H
    sblocks = seq // tm
    const = lambda i: (0, 0)
    head_major = pl.BlockSpec((1, H, tm, dh), lambda i: (i // sblocks, 0, i % sblocks, 0))
    shape = jax.ShapeDtypeStruct((batch, H, seq, dh), BF16)
    return pl.pallas_call(
        functools.partial(_sb_proj_kernel, dh=dh, q_scale=dh ** -0.5 * LOG2_E, group=4),
        grid=(T // tm,),
        in_specs=[
            pl.BlockSpec((tm, D), lambda i: (i, 0)),
            pl.BlockSpec((1, D), const),
            pl.BlockSpec((1, D), const),
            pl.BlockSpec(wkv.shape, const),
            pl.BlockSpec(wq.shape, const),
        ],
        out_specs=[head_major, head_major,
                   pl.BlockSpec((1, H, dh, tm), lambda i: (i // sblocks, 0, 0, i % sblocks))],
        out_shape=[shape, shape, jax.ShapeDtypeStruct((batch, H, dh, seq), BF16)],
        compiler_params=_params("parallel"),
        name="sb_proj",
    )(x2, gkv, gq, wkv, wq)


MASKED_LOG2 = -1e30
SB_PIPELINE_DEPTH = 4
SB_TABLE_FIELDS = 3
SB_DIAGONAL_UNROLL = 16


def _sb_streams(seq, tq, tk):
    band = tq // tk
    streams = [[], []]
    for qt in range(seq // tq):
        q0 = qt * tq
        for n in range(band + band * qt):
            streams[0 if n < band else 1].append((q0, q0 + (band - 1 - n) * tk, q0))
    counts = [len(s) for s in streams]
    flat, bases = [], []
    for s in streams:
        s = s + [(0, 0, seq)] * SB_PIPELINE_DEPTH
        bases.append(len(flat))
        for field in range(SB_TABLE_FIELDS):
            flat.extend(e[field] for e in s)
    return counts, bases, jnp.asarray(flat, jnp.int32)


def _sb_attn_kernel(tab_ref, q_ref, k_ref, vt_ref, tri_ref, o_ref, z_ref, sp_ref, tot_ref, m_ref, a_ref,
                    acct_ref, ls_ref, *, tq, tk, counts, bases, unroll):
    seq = q_ref.shape[2]
    lanes = tot_ref.shape[2]
    band = tq // tk
    depth = SB_PIPELINE_DEPTH
    assert band in (1, 2)
    sign_bit = jnp.uint32(0x80000000)

    def run_phase(p, slot, base, length, masked, last_stage=4):
        other = 1 - slot

        def field(f, stage):
            return tab_ref[base + f * length + (p - stage)]

        if last_stage >= 4:
            key0 = pl.multiple_of(field(1, 4), tk)
            rows = pl.ds(pl.multiple_of(field(2, 4), tq), tq)
            acct_ref[:, rows] += lax.dot_general(vt_ref[0, 0, :, pl.ds(key0, tk)], a_ref[other], NT_DIMS,
                                                 preferred_element_type=F32)
        if last_stage >= 3:
            rows = pl.ds(pl.multiple_of(field(2, 3), tq), tq)
            ls = ls_ref[rows, :]
            ls_ref[rows, :] = ls + tot_ref[slot]
            a = jnp.exp2(m_ref[other] - jnp.concatenate([ls] * (tk // lanes), axis=1))
            a_ref[slot] = a.astype(BF16)
        if last_stage >= 2:
            m_ref[slot] = z_ref[slot] - _dot(sp_ref[other], tri_ref[...])
        if last_stage >= 1:
            z = z_ref[other]
            neg_abs = pltpu.bitcast(pltpu.bitcast(z, jnp.uint32) | sign_bit, F32)
            sp = jnp.maximum(z, 0.0) + jnp.log2(1.0 + jnp.exp2(neg_abs))
            if masked:
                diag_shift = (band - 1 - (slot - 1) % band) * tk
                valid = (lax.broadcasted_iota(jnp.int32, (tq, tk), 1) + diag_shift
                         < lax.broadcasted_iota(jnp.int32, (tq, tk), 0))
                sp = jnp.where(valid, sp, 0.0)
                z_ref[other] = jnp.where(valid, z, MASKED_LOG2)
            sp_ref[slot] = sp.astype(BF16)
            tot_ref[slot] = jnp.broadcast_to(jnp.sum(sp, axis=-1, keepdims=True), (tq, lanes))
        qrow = pl.multiple_of(field(0, 0), tq)
        key0 = pl.multiple_of(field(1, 0), tk)
        z_ref[slot] = lax.dot_general(q_ref[0, 0, pl.ds(qrow, tq), :], k_ref[0, 0, pl.ds(key0, tk), :],
                                      NT_DIMS, preferred_element_type=F32)

    def run_stream(index, masked, unroll):
        count, base = counts[index], bases[index]
        length = count + depth
        unroll = max(u for u in range(2, unroll + 1, 2) if count % u == 0)
        assert depth % 2 == 0
        for p in range(depth):
            run_phase(p, p % 2, base, length, masked, last_stage=p)

        def steady(i, c):
            for j in range(unroll):
                run_phase(depth + unroll * i + j, j % 2, base, length, masked)
            return c

        lax.fori_loop(0, count // unroll, steady, 0)

    acct_ref[...] = jnp.zeros_like(acct_ref)
    ls_ref[...] = jnp.zeros_like(ls_ref)
    run_stream(0, True, min(unroll, SB_DIAGONAL_UNROLL))
    run_stream(1, False, unroll)

    o_ref[0, 0] = acct_ref[:, :seq].astype(o_ref.dtype)


def _sb_attn(q, k, vt, *, tq=256, tk=256, lanes=V7X_LANES, unroll=62):
    B, H, S, dh = q.shape
    idx = jnp.arange(tk)
    tri = (idx[:, None] >= idx[None, :]).astype(BF16)
    counts, bases, table = _sb_streams(S, tq, tk)
    per_head = pl.BlockSpec((1, 1, S, dh), lambda b, h, tab: (b, h, 0, 0))
    return pl.pallas_call(
        functools.partial(_sb_attn_kernel, tq=tq, tk=tk, counts=counts, bases=bases, unroll=unroll),
        grid_spec=pltpu.PrefetchScalarGridSpec(
            num_scalar_prefetch=1,
            grid=(B, H),
            in_specs=[per_head, per_head, pl.BlockSpec((1, 1, dh, S), lambda b, h, tab: (b, h, 0, 0)),
                      pl.BlockSpec(tri.shape, lambda b, h, tab: (0, 0))],
            out_specs=pl.BlockSpec((1, 1, dh, S), lambda b, h, tab: (b, h, 0, 0)),
            scratch_shapes=[
                pltpu.VMEM((2, tq, tk), F32),
                pltpu.VMEM((2, tq, tk), BF16),
                pltpu.VMEM((2, tq, lanes), F32),
                pltpu.VMEM((2, tq, tk), F32),
                pltpu.VMEM((2, tq, tk), BF16),
                pltpu.VMEM((dh, S + tq), F32),
                pltpu.VMEM((S + tq, lanes), F32),
            ],
        ),
        out_shape=jax.ShapeDtypeStruct((B, H, dh, S), BF16),
        compiler_params=_params("parallel", "parallel"),
        name="sb_attn",
    )(table, q, k, vt, tri)


def _rotary_tables(seq, dk):
    half = dk // 2
    inv_freq = ROPE_BASE ** (-jnp.arange(half, dtype=F32) / half)
    ang = jnp.arange(seq, dtype=F32)[:, None] * inv_freq[None, :]
    return jnp.cos(ang), jnp.sin(ang)


def kernel(x, mix_norm_g, w_ret_in, ret_gn_g, w_ret_out, kv_norm_g, w_kv, w_sb_q, w_sb_out,
           ffn_norm_g, w_ffn_in, w_ffn_out, final_norm_g):
    B, S, D = x.shape
    depth = mix_norm_g.shape[0]
    n_ret = w_ret_in.shape[0]
    assert w_sb_q.shape[0] == 1 and depth == n_ret + 1
    x2 = x.reshape(B * S, D)
    cos, sin = _rotary_tables(S, D // RET_HEADS)
    final_g = final_norm_g.reshape(1, D)

    for layer in range(depth):
        mix_g = mix_norm_g[layer].reshape(1, D)
        ffn_args = (ffn_norm_g[layer].reshape(1, D), w_ffn_in[layer].astype(BF16),
                    w_ffn_out[layer].astype(BF16), final_g)
        final_norm = layer == depth - 1
        if layer < n_ret:
            q, k, kdt, v, gate = _ret_proj(x2, mix_g, w_ret_in[layer].astype(BF16), cos, sin, seq=S)
            o = _retention(q, k, kdt, v, batch=B, seq=S)
            x2 = _ret_out_ffn(x2, o, gate, ret_gn_g[layer].reshape(1, -1), w_ret_out[layer].astype(BF16),
                              *ffn_args, final_norm=final_norm)
        else:
            q, k_sb, vt_sb = _sb_proj(x2, kv_norm_g.reshape(1, D), mix_g,
                                     w_kv.astype(BF16), w_sb_q[0].astype(BF16), batch=B, seq=S)
            o = _sb_attn(q, k_sb, vt_sb)
            x2 = _sb_out_ffn(x2, o, w_sb_out[0].astype(BF16), *ffn_args, seq=S, final_norm=final_norm)
    return x2.reshape(B, S, D)
```

```python
import functools

import jax
import jax.numpy as jnp
from jax import lax
from jax.experimental import pallas as pl
from jax.experimental.pallas import tpu as pltpu

RMS_EPS = 1e-6
GN_EPS = 1e-5
RET_HEADS = 4
RET_CHUNK = 128
ROPE_BASE = 10000.0
SB_HEADS = 16

LOG2_E = 1.4426950408889634

F32 = jnp.float32
BF16 = jnp.bfloat16

V7X_VMEM_BYTES = 64 * 1024 * 1024
VMEM_LIMIT_BYTES = (V7X_VMEM_BYTES * 7) // 8
V7X_LANES = 128

NT_DIMS = (((1,), (1,)), ((), ()))
TN_DIMS = (((0,), (0,)), ((), ()))


def _params(*semantics):
    return pltpu.CompilerParams(dimension_semantics=semantics, vmem_limit_bytes=VMEM_LIMIT_BYTES)


def _rms_scale(x):
    return x * lax.rsqrt(jnp.mean(x * x, axis=-1, keepdims=True) + RMS_EPS)


def _silu(g):
    return g * (1.0 / (1.0 + jnp.exp(-g)))


def _dot(a, b):
    return jnp.dot(a, b, preferred_element_type=F32)


def _ret_proj_kernel(x_ref, g_ref, w_ref, cos_ref, sin_ref, kdec_ref, q_ref, k_ref, kdt_ref, v_ref, gate_ref,
                     *, dk, k_scale):
    hb = (_rms_scale(x_ref[...]) * g_ref[...]).astype(BF16)
    cos = cos_ref[...]
    sin = sin_ref[...]
    half = dk // 2
    qk_width = q_ref.shape[1]
    v_width = v_ref.shape[1]
    for base, out_ref, scale in ((0, q_ref, 1.0), (qk_width, k_ref, k_scale)):
        for c in range(0, qk_width, dk):
            t = _dot(hb, w_ref[:, base + c:base + c + dk])
            t1, t2 = t[:, :half], t[:, half:]
            r1 = (t1 * cos - t2 * sin) * scale
            r2 = (t1 * sin + t2 * cos) * scale
            out_ref[:, c:c + half] = r1.astype(BF16)
            out_ref[:, c + half:c + dk] = r2.astype(BF16)
            if out_ref is k_ref:
                kdec = kdec_ref[c // dk]
                kdt_ref[c:c + half, :] = (r1 * kdec).T.astype(BF16)
                kdt_ref[c + half:c + dk, :] = (r2 * kdec).T.astype(BF16)
    for base, out_ref in ((2 * qk_width, v_ref), (2 * qk_width + v_width, gate_ref)):
        for c in range(0, v_width, 2 * dk):
            out_ref[:, c:c + 2 * dk] = _dot(hb, w_ref[:, base + c:base + c + 2 * dk]).astype(BF16)


def _ret_proj(x2, g, w, cos, sin, *, seq, tm=512):
    T, D = x2.shape
    dk = D // RET_HEADS
    qk_width, v_width = D, 2 * D
    sblocks = seq // tm
    row = lambda i: (i, 0)
    kdec = jnp.tile(_retention_decays()[2], (1, tm // RET_CHUNK, 1))
    return pl.pallas_call(
        functools.partial(_ret_proj_kernel, dk=dk, k_scale=dk ** -0.5),
        grid=(T // tm,),
        in_specs=[
            pl.BlockSpec((tm, D), row),
            _resident((1, D)),
            _resident(w.shape),
            pl.BlockSpec((tm, dk // 2), lambda i: (i % sblocks, 0)),
            pl.BlockSpec((tm, dk // 2), lambda i: (i % sblocks, 0)),
            _resident(kdec.shape),
        ],
        out_specs=[
            pl.BlockSpec((tm, qk_width), row),
            pl.BlockSpec((tm, qk_width), row),
            pl.BlockSpec((qk_width, tm), lambda i: (0, i)),
            pl.BlockSpec((tm, v_width), row),
            pl.BlockSpec((tm, v_width), row),
        ],
        out_shape=[
            jax.ShapeDtypeStruct((T, qk_width), BF16),
            jax.ShapeDtypeStruct((T, qk_width), BF16),
            jax.ShapeDtypeStruct((qk_width, T), BF16),
            jax.ShapeDtypeStruct((T, v_width), BF16),
            jax.ShapeDtypeStruct((T, v_width), BF16),
        ],
        compiler_params=_params("parallel"),
        name="ret_proj",
    )(x2, g, w, cos, sin, kdec)


def _retention_kernel(q_ref, k_ref, kdt_ref, v_ref, dmat_ref, qdec_ref, cdec_ref, o_ref, state_ref, *, chunk):
    @pl.when(pl.program_id(2) == 0)
    def _():
        state_ref[...] = jnp.zeros_like(state_ref)

    dmat = dmat_ref[0]
    qdec = qdec_ref[0]
    cdec = cdec_ref[0]
    n_chunks = q_ref.shape[0] // chunk
    chunks = [slice(c * chunk, (c + 1) * chunk) for c in range(n_chunks)]
    updates = [_dot(kdt_ref[:, rows], v_ref[rows, :]) for rows in chunks]
    states = [state_ref[...]]
    for upd in updates:
        states.append(states[-1] * cdec + upd)
    state_ref[...] = states[-1]
    for rows, state in zip(chunks, states):
        qc = q_ref[rows, :]
        vc = v_ref[rows, :]
        scores = lax.dot_general(qc, k_ref[rows, :], NT_DIMS, preferred_element_type=F32) * dmat
        o_ref[rows, :] = _dot(scores.astype(BF16), vc) + _dot(qc, state.astype(BF16)) * qdec


def _retention_decays():
    H, C = RET_HEADS, RET_CHUNK
    log_g = jnp.log(1.0 - 2.0 ** (-5.0 - jnp.arange(H, dtype=F32)))
    j = jnp.arange(C, dtype=F32)
    rel = j[:, None] - j[None, :]
    dmat = jnp.where(rel[None] >= 0, jnp.exp(rel[None] * log_g[:, None, None]), 0.0)
    qdec = jnp.exp((j[None, :] + 1.0) * log_g[:, None])[:, :, None]
    kdec = jnp.exp((C - 1.0 - j[None, :]) * log_g[:, None])[:, :, None]
    cdec = jnp.exp(C * log_g)[:, None, None]
    return dmat, qdec, kdec, cdec


def _retention(q, k, kdt, v, *, batch, seq, rows_per_step=2048):
    T = q.shape[0]
    rows_per_step = min(rows_per_step, seq)
    H, C = RET_HEADS, RET_CHUNK
    dk = q.shape[1] // H
    dv = v.shape[1] // H
    nsb = seq // rows_per_step
    dmat, qdec, _, cdec = _retention_decays()

    tok = lambda b, h, s: (b * nsb + s, h)
    per_head = lambda b, h, s: (h, 0, 0)
    return pl.pallas_call(
        functools.partial(_retention_kernel, chunk=C),
        grid=(batch, H, nsb),
        in_specs=[
            pl.BlockSpec((rows_per_step, dk), tok),
            pl.BlockSpec((rows_per_step, dk), tok),
            pl.BlockSpec((dk, rows_per_step), lambda b, h, s: (h, b * nsb + s)),
            pl.BlockSpec((rows_per_step, dv), tok),
            pl.BlockSpec((1, C, C), per_head),
            pl.BlockSpec((1, C, 1), per_head),
            pl.BlockSpec((1, 1, 1), per_head),
        ],
        out_specs=pl.BlockSpec((rows_per_step, dv), tok),
        out_shape=jax.ShapeDtypeStruct((T, v.shape[1]), F32),
        scratch_shapes=[pltpu.VMEM((dk, dv), F32)],
        compiler_params=_params("parallel", "parallel", "arbitrary"),
        name="retention",
    )(q, k, kdt, v, dmat, qdec, cdec)


def _ffn_tail(x1, g_ref, win_ref, wout_ref, fg_ref, out_ref, *, chunk, final_norm):
    d_ff = wout_ref.shape[0]
    hb = (_rms_scale(x1) * g_ref[...]).astype(BF16)
    out_ref[...] = x1
    for c in range(0, d_ff, chunk):
        gate = _dot(hb, win_ref[:, c:c + chunk])
        up = _dot(hb, win_ref[:, d_ff + c:d_ff + c + chunk])
        act = (_silu(gate) * up).astype(BF16)
        out_ref[...] += _dot(act, wout_ref[c:c + chunk, :])
    if final_norm:
        out_ref[...] = _rms_scale(out_ref[...]) * fg_ref[...]


def _ret_out_ffn_kernel(x_ref, o_ref, gate_ref, gn_ref, wo_ref, g_ref, win_ref, wout_ref, fg_ref, out_ref,
                        *, dv, chunk, final_norm):
    x1 = x_ref[...]
    for c in range(0, o_ref.shape[1], dv):
        o = o_ref[:, c:c + dv]
        oc = o - jnp.mean(o, axis=-1, keepdims=True)
        on = oc * lax.rsqrt(jnp.mean(oc * oc, axis=-1, keepdims=True) + GN_EPS) * gn_ref[:, c:c + dv]
        y = (_silu(gate_ref[:, c:c + dv].astype(F32)) * on).astype(BF16)
        x1 = x1 + _dot(y, wo_ref[c:c + dv, :])
    _ffn_tail(x1, g_ref, win_ref, wout_ref, fg_ref, out_ref, chunk=chunk, final_norm=final_norm)


def _sb_out_ffn_kernel(x_ref, ot_ref, wo_ref, g_ref, win_ref, wout_ref, fg_ref, out_ref, *, chunk, final_norm):
    H, dh, tm = ot_ref.shape[1:]
    heads_t = ot_ref[0].reshape(H * dh, tm)
    x1 = x_ref[...] + lax.dot_general(heads_t, wo_ref[...], TN_DIMS, preferred_element_type=F32)
    _ffn_tail(x1, g_ref, win_ref, wout_ref, fg_ref, out_ref, chunk=chunk, final_norm=final_norm)


def _resident(shape):
    return pl.BlockSpec(shape, lambda i: (0,) * len(shape), pipeline_mode=pl.Buffered(1))


def _mixer_out_ffn(body, x2, mixer_specs, mixer_args, g, w_in, w_out, final_g, *, name, tm):
    T, D = x2.shape
    row = lambda i: (i, 0)
    return pl.pallas_call(
        body,
        grid=(T // tm,),
        in_specs=[pl.BlockSpec((tm, D), row)] + mixer_specs + [
            _resident((1, D)), _resident(w_in.shape), _resident(w_out.shape), _resident((1, D))],
        out_specs=pl.BlockSpec((tm, D), row),
        out_shape=jax.ShapeDtypeStruct((T, D), F32),
        compiler_params=_params("parallel"),
        name=name,
    )(x2, *mixer_args, g, w_in, w_out, final_g)


def _ret_out_ffn(x2, o, gate, gn_g, wo, g, w_in, w_out, final_g, *, final_norm, tm=512, chunk=256):
    width = o.shape[1]
    row = lambda i: (i, 0)
    body = functools.partial(_ret_out_ffn_kernel, dv=width // RET_HEADS, chunk=chunk, final_norm=final_norm)
    specs = [pl.BlockSpec((tm, width), row), pl.BlockSpec((tm, width), row), _resident((1, width)),
             _resident(wo.shape)]
    return _mixer_out_ffn(body, x2, specs, (o, gate, gn_g, wo), g, w_in, w_out, final_g,
                          name="ret_out_ffn", tm=tm)


def _sb_out_ffn(x2, o, wo, g, w_in, w_out, final_g, *, seq, final_norm, tm=512, chunk=256):
    _, H, dh, _ = o.shape
    sblocks = seq // tm
    body = functools.partial(_sb_out_ffn_kernel, chunk=chunk, final_norm=final_norm)
    specs = [pl.BlockSpec((1, H, dh, tm), lambda i: (i // sblocks, 0, 0, i % sblocks)), _resident(wo.shape)]
    return _mixer_out_ffn(body, x2, specs, (o, wo), g, w_in, w_out, final_g, name="sb_out_ffn", tm=tm)


def _sb_proj_kernel(x_ref, gkv_ref, gq_ref, wkv_ref, wq_ref, q_ref, k_ref, vt_ref, *, dh, q_scale, group):
    r = _rms_scale(x_ref[...])
    hkv = (r * gkv_ref[...]).astype(BF16)
    hq = (r * gq_ref[...]).astype(BF16)
    D = wq_ref.shape[1]
    width = group * dh
    for c in range(0, D, width):
        tq = _dot(hq, wq_ref[:, c:c + width]) * q_scale
        tk = _dot(hkv, wkv_ref[:, c:c + width])
        tv = _dot(hkv, wkv_ref[:, D + c:D + c + width])
        for g in range(group):
            h = c // dh + g
            lanes = slice(g * dh, (g + 1) * dh)
            q_ref[0, h] = tq[:, lanes].astype(BF16)
            k_ref[0, h] = tk[:, lanes].astype(BF16)
            vt_ref[0, h] = tv[:, lanes].T.astype(BF16)


def _sb_proj(x2, gkv, gq, wkv, wq, *, batch, seq, tm=512):
    T, D = x2.shape
    H = SB_HEADS
    dh = D // H
    sblocks = seq // tm
    const = lambda i: (0, 0)
    head_major = pl.BlockSpec((1, H, tm, dh), lambda i: (i // sblocks, 0, i % sblocks, 0))
    shape = jax.ShapeDtypeStruct((batch, H, seq, dh), BF16)
    return pl.pallas_call(
        functools.partial(_sb_proj_kernel, dh=dh, q_scale=dh ** -0.5 * LOG2_E, group=4),
        grid=(T // tm,),
        in_specs=[
            pl.BlockSpec((tm, D), lambda i: (i, 0)),
            pl.BlockSpec((1, D), const),
            pl.BlockSpec((1, D), const),
            pl.BlockSpec(wkv.shape, const),
            pl.BlockSpec(wq.shape, const),
        ],
        out_specs=[head_major, head_major,
                   pl.BlockSpec((1, H, dh, tm), lambda i: (i // sblocks, 0, 0, i % sblocks))],
        out_shape=[shape, shape, jax.ShapeDtypeStruct((batch, H, dh, seq), BF16)],
        compiler_params=_params("parallel"),
        name="sb_proj",
    )(x2, gkv, gq, wkv, wq)


MASKED_LOG2 = -1e30
SOFTPLUS2_DIRECT_MAX = 126.0
SB_PIPELINE_DEPTH = 4
SB_TABLE_FIELDS = 3
SB_DIAGONAL_UNROLL = 16


def _sb_streams(seq, tq, tk):
    band = tq // tk
    streams = [[], []]
    for qt in range(seq // tq):
        q0 = qt * tq
        for n in range(band + band * qt):
            streams[0 if n < band else 1].append((q0, q0 + (band - 1 - n) * tk, q0))
    counts = [len(s) for s in streams]
    flat, bases = [], []
    for s in streams:
        s = s + [(0, 0, seq)] * SB_PIPELINE_DEPTH
        bases.append(len(flat))
        for field in range(SB_TABLE_FIELDS):
            flat.extend(e[field] for e in s)
    return counts, bases, jnp.asarray(flat, jnp.int32)


def _sb_attn_kernel(tab_ref, q_ref, k_ref, vt_ref, tri_ref, o_ref, z_ref, sp_ref, tot_ref, m_ref, a_ref,
                    acct_ref, ls_ref, *, tq, tk, counts, bases, unroll):
    seq = q_ref.shape[2]
    lanes = tot_ref.shape[2]
    band = tq // tk
    depth = SB_PIPELINE_DEPTH
    assert band in (1, 2)

    def run_phase(p, slot, base, length, masked, last_stage=4):
        other = 1 - slot

        def field(f, stage):
            return tab_ref[base + f * length + (p - stage)]

        if last_stage >= 4:
            key0 = pl.multiple_of(field(1, 4), tk)
            rows = pl.ds(pl.multiple_of(field(2, 4), tq), tq)
            acct_ref[:, rows] += lax.dot_general(vt_ref[0, 0, :, pl.ds(key0, tk)], a_ref[other], NT_DIMS,
                                                 preferred_element_type=F32)
        if last_stage >= 3:
            rows = pl.ds(pl.multiple_of(field(2, 3), tq), tq)
            ls = ls_ref[rows, :]
            ls_ref[rows, :] = ls + tot_ref[slot]
            a = jnp.exp2(m_ref[other] - jnp.concatenate([ls] * (tk // lanes), axis=1))
            a_ref[slot] = a.astype(BF16)
        if last_stage >= 2:
            m_ref[slot] = z_ref[slot] - _dot(sp_ref[other], tri_ref[...])
        if last_stage >= 1:
            z = z_ref[other]
            sp = jnp.maximum(jnp.log2(1.0 + jnp.exp2(jnp.minimum(z, SOFTPLUS2_DIRECT_MAX))), z)
            if masked:
                diag_shift = (band - 1 - (slot - 1) % band) * tk
                valid = (lax.broadcasted_iota(jnp.int32, (tq, tk), 1) + diag_shift
                         < lax.broadcasted_iota(jnp.int32, (tq, tk), 0))
                sp = jnp.where(valid, sp, 0.0)
                z_ref[other] = jnp.where(valid, z, MASKED_LOG2)
            sp_ref[slot] = sp.astype(BF16)
            tot_ref[slot] = jnp.broadcast_to(jnp.sum(sp, axis=-1, keepdims=True), (tq, lanes))
        qrow = pl.multiple_of(field(0, 0), tq)
        key0 = pl.multiple_of(field(1, 0), tk)
        z_ref[slot] = lax.dot_general(q_ref[0, 0, pl.ds(qrow, tq), :], k_ref[0, 0, pl.ds(key0, tk), :],
                                      NT_DIMS, preferred_element_type=F32)

    def run_stream(index, masked, unroll):
        count, base = counts[index], bases[index]
        length = count + depth
        unroll = max(u for u in range(2, unroll + 1, 2) if count % u == 0)
        assert depth % 2 == 0
        for p in range(depth):
            run_phase(p, p % 2, base, length, masked, last_stage=p)

        def steady(i, c):
            for j in range(unroll):
                run_phase(depth + unroll * i + j, j % 2, base, length, masked)
            return c

        lax.fori_loop(0, count // unroll, steady, 0)

    acct_ref[...] = jnp.zeros_like(acct_ref)
    ls_ref[...] = jnp.zeros_like(ls_ref)
    run_stream(0, True, min(unroll, SB_DIAGONAL_UNROLL))
    run_stream(1, False, unroll)

    o_ref[0, 0] = acct_ref[:, :seq].astype(o_ref.dtype)


def _sb_attn(q, k, vt, *, tq=256, tk=256, lanes=V7X_LANES, unroll=62):
    B, H, S, dh = q.shape
    idx = jnp.arange(tk)
    tri = (idx[:, None] >= idx[None, :]).astype(BF16)
    counts, bases, table = _sb_streams(S, tq, tk)
    per_head = pl.BlockSpec((1, 1, S, dh), lambda b, h, tab: (b, h, 0, 0))
    return pl.pallas_call(
        functools.partial(_sb_attn_kernel, tq=tq, tk=tk, counts=counts, bases=bases, unroll=unroll),
        grid_spec=pltpu.PrefetchScalarGridSpec(
            num_scalar_prefetch=1,
            grid=(B, H),
            in_specs=[per_head, per_head, pl.BlockSpec((1, 1, dh, S), lambda b, h, tab: (b, h, 0, 0)),
                      pl.BlockSpec(tri.shape, lambda b, h, tab: (0, 0))],
            out_specs=pl.BlockSpec((1, 1, dh, S), lambda b, h, tab: (b, h, 0, 0)),
            scratch_shapes=[
                pltpu.VMEM((2, tq, tk), F32),
                pltpu.VMEM((2, tq, tk), BF16),
                pltpu.VMEM((2, tq, lanes), F32),
                pltpu.VMEM((2, tq, tk), F32),
                pltpu.VMEM((2, tq, tk), BF16),
                pltpu.VMEM((dh, S + tq), F32),
                pltpu.VMEM((S + tq, lanes), F32),
            ],
        ),
        out_shape=jax.ShapeDtypeStruct((B, H, dh, S), BF16),
        compiler_params=_params("parallel", "parallel"),
        name="sb_attn",
    )(table, q, k, vt, tri)


def _rotary_tables(seq, dk):
    half = dk // 2
    inv_freq = ROPE_BASE ** (-jnp.arange(half, dtype=F32) / half)
    ang = jnp.arange(seq, dtype=F32)[:, None] * inv_freq[None, :]
    return jnp.cos(ang), jnp.sin(ang)


def kernel(x, mix_norm_g, w_ret_in, ret_gn_g, w_ret_out, kv_norm_g, w_kv, w_sb_q, w_sb_out,
           ffn_norm_g, w_ffn_in, w_ffn_out, final_norm_g):
    B, S, D = x.shape
    depth = mix_norm_g.shape[0]
    n_ret = w_ret_in.shape[0]
    assert w_sb_q.shape[0] == 1 and depth == n_ret + 1
    x2 = x.reshape(B * S, D)
    cos, sin = _rotary_tables(S, D // RET_HEADS)
    final_g = final_norm_g.reshape(1, D)

    for layer in range(depth):
        mix_g = mix_norm_g[layer].reshape(1, D)
        ffn_args = (ffn_norm_g[layer].reshape(1, D), w_ffn_in[layer].astype(BF16),
                    w_ffn_out[layer].astype(BF16), final_g)
        final_norm = layer == depth - 1
        if layer < n_ret:
            q, k, kdt, v, gate = _ret_proj(x2, mix_g, w_ret_in[layer].astype(BF16), cos, sin, seq=S)
            o = _retention(q, k, kdt, v, batch=B, seq=S)
            x2 = _ret_out_ffn(x2, o, gate, ret_gn_g[layer].reshape(1, -1), w_ret_out[layer].astype(BF16),
                              *ffn_args, final_norm=final_norm)
        else:
            q, k_sb, vt_sb = _sb_proj(x2, kv_norm_g.reshape(1, D), mix_g,
                                     w_kv.astype(BF16), w_sb_q[0].astype(BF16), batch=B, seq=S)
            o = _sb_attn(q, k_sb, vt_sb)
            x2 = _sb_out_ffn(x2, o, w_sb_out[0].astype(BF16), *ffn_args, seq=S, final_norm=final_norm)
    return x2.reshape(B, S, D)
```

```python
import functools

import jax
import jax.numpy as jnp
from jax import lax
from jax.experimental import pallas as pl
from jax.experimental.pallas import tpu as pltpu

RMS_EPS = 1e-6
GN_EPS = 1e-5
RET_HEADS = 4
RET_CHUNK = 256
ROPE_BASE = 10000.0
SB_HEADS = 16

LOG2_E = 1.4426950408889634

F32 = jnp.float32
BF16 = jnp.bfloat16

V7X_VMEM_BYTES = 64 * 1024 * 1024
VMEM_LIMIT_BYTES = (V7X_VMEM_BYTES * 7) // 8
V7X_LANES = 128

NT_DIMS = (((1,), (1,)), ((), ()))
TN_DIMS = (((0,), (0,)), ((), ()))


def _params(*semantics):
    return pltpu.CompilerParams(dimension_semantics=semantics, vmem_limit_bytes=VMEM_LIMIT_BYTES)


def _rms_scale(x):
    return x * lax.rsqrt(jnp.mean(x * x, axis=-1, keepdims=True) + RMS_EPS)


def _silu(g):
    return g * (1.0 / (1.0 + jnp.exp(-g)))


def _dot(a, b):
    return jnp.dot(a, b, preferred_element_type=F32)


def _ret_proj_kernel(x_ref, g_ref, w_ref, cos_ref, sin_ref, kdec_ref, q_ref, k_ref, kdt_ref, v_ref, gate_ref,
                     *, dk, k_scale):
    hb = (_rms_scale(x_ref[...]) * g_ref[...]).astype(BF16)
    cos = cos_ref[...]
    sin = sin_ref[...]
    half = dk // 2
    qk_width = q_ref.shape[1]
    v_width = v_ref.shape[1]
    for base, out_ref, scale in ((0, q_ref, 1.0), (qk_width, k_ref, k_scale)):
        for c in range(0, qk_width, dk):
            t = _dot(hb, w_ref[:, base + c:base + c + dk])
            t1, t2 = t[:, :half], t[:, half:]
            r1 = (t1 * cos - t2 * sin) * scale
            r2 = (t1 * sin + t2 * cos) * scale
            out_ref[:, c:c + half] = r1.astype(BF16)
            out_ref[:, c + half:c + dk] = r2.astype(BF16)
            if out_ref is k_ref:
                kdec = kdec_ref[c // dk]
                kdt_ref[c:c + half, :] = (r1 * kdec).T.astype(BF16)
                kdt_ref[c + half:c + dk, :] = (r2 * kdec).T.astype(BF16)
    for base, out_ref in ((2 * qk_width, v_ref), (2 * qk_width + v_width, gate_ref)):
        for c in range(0, v_width, 2 * dk):
            out_ref[:, c:c + 2 * dk] = _dot(hb, w_ref[:, base + c:base + c + 2 * dk]).astype(BF16)


def _ret_proj(x2, g, w, cos, sin, *, seq, tm=512):
    T, D = x2.shape
    dk = D // RET_HEADS
    qk_width, v_width = D, 2 * D
    sblocks = seq // tm
    row = lambda i: (i, 0)
    kdec = jnp.tile(_retention_decays()[2], (1, tm // RET_CHUNK, 1))
    return pl.pallas_call(
        functools.partial(_ret_proj_kernel, dk=dk, k_scale=dk ** -0.5),
        grid=(T // tm,),
        in_specs=[
            pl.BlockSpec((tm, D), row),
            _resident((1, D)),
            _resident(w.shape),
            pl.BlockSpec((tm, dk // 2), lambda i: (i % sblocks, 0)),
            pl.BlockSpec((tm, dk // 2), lambda i: (i % sblocks, 0)),
            _resident(kdec.shape),
        ],
        out_specs=[
            pl.BlockSpec((tm, qk_width), row),
            pl.BlockSpec((tm, qk_width), row),
            pl.BlockSpec((qk_width, tm), lambda i: (0, i)),
            pl.BlockSpec((tm, v_width), row),
            pl.BlockSpec((tm, v_width), row),
        ],
        out_shape=[
            jax.ShapeDtypeStruct((T, qk_width), BF16),
            jax.ShapeDtypeStruct((T, qk_width), BF16),
            jax.ShapeDtypeStruct((qk_width, T), BF16),
            jax.ShapeDtypeStruct((T, v_width), BF16),
            jax.ShapeDtypeStruct((T, v_width), BF16),
        ],
        compiler_params=_params("parallel"),
        name="ret_proj",
    )(x2, g, w, cos, sin, kdec)


def _retention_kernel(q_ref, k_ref, kdt_ref, v_ref, dmat_ref, qdec_ref, cdec_ref, o_ref, state_ref, *, chunk):
    @pl.when(pl.program_id(2) == 0)
    def _():
        state_ref[...] = jnp.zeros_like(state_ref)

    dmat = dmat_ref[0]
    qdec = qdec_ref[0]
    cdec = cdec_ref[0]
    n_chunks = q_ref.shape[0] // chunk
    chunks = [slice(c * chunk, (c + 1) * chunk) for c in range(n_chunks)]
    updates = [_dot(kdt_ref[:, rows], v_ref[rows, :]) for rows in chunks]
    states = [state_ref[...]]
    for upd in updates:
        states.append(states[-1] * cdec + upd)
    state_ref[...] = states[-1]
    for rows, state in zip(chunks, states):
        qc = q_ref[rows, :]
        vc = v_ref[rows, :]
        scores = lax.dot_general(qc, k_ref[rows, :], NT_DIMS, preferred_element_type=F32) * dmat
        o_ref[rows, :] = _dot(scores.astype(BF16), vc) + _dot(qc, state.astype(BF16)) * qdec


def _retention_decays():
    H, C = RET_HEADS, RET_CHUNK
    log_g = jnp.log(1.0 - 2.0 ** (-5.0 - jnp.arange(H, dtype=F32)))
    j = jnp.arange(C, dtype=F32)
    rel = j[:, None] - j[None, :]
    dmat = jnp.where(rel[None] >= 0, jnp.exp(rel[None] * log_g[:, None, None]), 0.0)
    qdec = jnp.exp((j[None, :] + 1.0) * log_g[:, None])[:, :, None]
    kdec = jnp.exp((C - 1.0 - j[None, :]) * log_g[:, None])[:, :, None]
    cdec = jnp.exp(C * log_g)[:, None, None]
    return dmat, qdec, kdec, cdec


def _retention(q, k, kdt, v, *, batch, seq, rows_per_step=2048):
    T = q.shape[0]
    rows_per_step = min(rows_per_step, seq)
    H, C = RET_HEADS, RET_CHUNK
    dk = q.shape[1] // H
    dv = v.shape[1] // H
    nsb = seq // rows_per_step
    dmat, qdec, _, cdec = _retention_decays()

    tok = lambda b, h, s: (b * nsb + s, h)
    per_head = lambda b, h, s: (h, 0, 0)
    return pl.pallas_call(
        functools.partial(_retention_kernel, chunk=C),
        grid=(batch, H, nsb),
        in_specs=[
            pl.BlockSpec((rows_per_step, dk), tok),
            pl.BlockSpec((rows_per_step, dk), tok),
            pl.BlockSpec((dk, rows_per_step), lambda b, h, s: (h, b * nsb + s)),
            pl.BlockSpec((rows_per_step, dv), tok),
            pl.BlockSpec((1, C, C), per_head),
            pl.BlockSpec((1, C, 1), per_head),
            pl.BlockSpec((1, 1, 1), per_head),
        ],
        out_specs=pl.BlockSpec((rows_per_step, dv), tok),
        out_shape=jax.ShapeDtypeStruct((T, v.shape[1]), F32),
        scratch_shapes=[pltpu.VMEM((dk, dv), F32)],
        compiler_params=_params("parallel", "parallel", "arbitrary"),
        name="retention",
    )(q, k, kdt, v, dmat, qdec, cdec)


def _ffn_tail(x1, g_ref, win_ref, wout_ref, fg_ref, out_ref, *, chunk, final_norm):
    d_ff = wout_ref.shape[0]
    hb = (_rms_scale(x1) * g_ref[...]).astype(BF16)
    out_ref[...] = x1
    for c in range(0, d_ff, chunk):
        gate = _dot(hb, win_ref[:, c:c + chunk])
        up = _dot(hb, win_ref[:, d_ff + c:d_ff + c + chunk])
        act = (_silu(gate) * up).astype(BF16)
        out_ref[...] += _dot(act, wout_ref[c:c + chunk, :])
    if final_norm:
        out_ref[...] = _rms_scale(out_ref[...]) * fg_ref[...]


def _ret_out_ffn_kernel(x_ref, o_ref, gate_ref, gn_ref, wo_ref, g_ref, win_ref, wout_ref, fg_ref, out_ref,
                        *, dv, chunk, final_norm):
    x1 = x_ref[...]
    for c in range(0, o_ref.shape[1], dv):
        o = o_ref[:, c:c + dv]
        oc = o - jnp.mean(o, axis=-1, keepdims=True)
        on = oc * lax.rsqrt(jnp.mean(oc * oc, axis=-1, keepdims=True) + GN_EPS) * gn_ref[:, c:c + dv]
        y = (_silu(gate_ref[:, c:c + dv].astype(F32)) * on).astype(BF16)
        x1 = x1 + _dot(y, wo_ref[c:c + dv, :])
    _ffn_tail(x1, g_ref, win_ref, wout_ref, fg_ref, out_ref, chunk=chunk, final_norm=final_norm)


def _sb_out_ffn_kernel(x_ref, ot_ref, wo_ref, g_ref, win_ref, wout_ref, fg_ref, out_ref, *, chunk, final_norm):
    H, dh, tm = ot_ref.shape[1:]
    heads_t = ot_ref[0].reshape(H * dh, tm)
    x1 = x_ref[...] + lax.dot_general(heads_t, wo_ref[...], TN_DIMS, preferred_element_type=F32)
    _ffn_tail(x1, g_ref, win_ref, wout_ref, fg_ref, out_ref, chunk=chunk, final_norm=final_norm)


def _resident(shape):
    return pl.BlockSpec(shape, lambda i: (0,) * len(shape), pipeline_mode=pl.Buffered(1))


def _mixer_out_ffn(body, x2, mixer_specs, mixer_args, g, w_in, w_out, final_g, *, name, tm):
    T, D = x2.shape
    row = lambda i: (i, 0)
    return pl.pallas_call(
        body,
        grid=(T // tm,),
        in_specs=[pl.BlockSpec((tm, D), row)] + mixer_specs + [
            _resident((1, D)), _resident(w_in.shape), _resident(w_out.shape), _resident((1, D))],
        out_specs=pl.BlockSpec((tm, D), row),
        out_shape=jax.ShapeDtypeStruct((T, D), F32),
        compiler_params=_params("parallel"),
        name=name,
    )(x2, *mixer_args, g, w_in, w_out, final_g)


def _ret_out_ffn(x2, o, gate, gn_g, wo, g, w_in, w_out, final_g, *, final_norm, tm=512, chunk=256):
    width = o.shape[1]
    row = lambda i: (i, 0)
    body = functools.partial(_ret_out_ffn_kernel, dv=width // RET_HEADS, chunk=chunk, final_norm=final_norm)
    specs = [pl.BlockSpec((tm, width), row), pl.BlockSpec((tm, width), row), _resident((1, width)),
             _resident(wo.shape)]
    return _mixer_out_ffn(body, x2, specs, (o, gate, gn_g, wo), g, w_in, w_out, final_g,
                          name="ret_out_ffn", tm=tm)


def _sb_out_ffn(x2, o, wo, g, w_in, w_out, final_g, *, seq, final_norm, tm=512, chunk=256):
    _, H, dh, _ = o.shape
    sblocks = seq // tm
    body = functools.partial(_sb_out_ffn_kernel, chunk=chunk, final_norm=final_norm)
    specs = [pl.BlockSpec((1, H, dh, tm), lambda i: (i // sblocks, 0, 0, i % sblocks)), _resident(wo.shape)]
    return _mixer_out_ffn(body, x2, specs, (o, wo), g, w_in, w_out, final_g, name="sb_out_ffn", tm=tm)


def _sb_proj_kernel(x_ref, gkv_ref, gq_ref, wkv_ref, wq_ref, q_ref, k_ref, vt_ref, *, dh, q_scale, group):
    r = _rms_scale(x_ref[...])
    hkv = (r * gkv_ref[...]).astype(BF16)
    hq = (r * gq_ref[...]).astype(BF16)
    D = wq_ref.shape[1]
    width = group * dh
    for c in range(0, D, width):
        tq = _dot(hq, wq_ref[:, c:c + width]) * q_scale
        tk = _dot(hkv, wkv_ref[:, c:c + width])
        tv = _dot(hkv, wkv_ref[:, D + c:D + c + width])
        for g in range(group):
            h = c // dh + g
            lanes = slice(g * dh, (g + 1) * dh)
            q_ref[0, h] = tq[:, lanes].astype(BF16)
            k_ref[0, h] = tk[:, lanes].astype(BF16)
            vt_ref[0, h] = tv[:, lanes].T.astype(BF16)


def _sb_proj(x2, gkv, gq, wkv, wq, *, batch, seq, tm=512):
    T, D = x2.shape
    H = SB_HEADS
    dh = D // H
    sblocks = seq // tm
    const = lambda i: (0, 0)
    head_major = pl.BlockSpec((1, H, tm, dh), lambda i: (i // sblocks, 0, i % sblocks, 0))
    shape = jax.ShapeDtypeStruct((batch, H, seq, dh), BF16)
    return pl.pallas_call(
        functools.partial(_sb_proj_kernel, dh=dh, q_scale=dh ** -0.5 * LOG2_E, group=4),
        grid=(T // tm,),
        in_specs=[
            pl.BlockSpec((tm, D), lambda i: (i, 0)),
            pl.BlockSpec((1, D), const),
            pl.BlockSpec((1, D), const),
            pl.BlockSpec(wkv.shape, const),
            pl.BlockSpec(wq.shape, const),
        ],
        out_specs=[head_major, head_major,
                   pl.BlockSpec((1, H, dh, tm), lambda i: (i // sblocks, 0, 0, i % sblocks))],
        out_shape=[shape, shape, jax.ShapeDtypeStruct((batch, H, dh, seq), BF16)],
        compiler_params=_params("parallel"),
        name="sb_proj",
    )(x2, gkv, gq, wkv, wq)


MASKED_LOG2 = -1e30
SOFTPLUS2_DIRECT_MAX = 126.0
SB_PIPELINE_DEPTH = 4
SB_TABLE_FIELDS = 3
SB_DIAGONAL_UNROLL = 16


def _sb_streams(seq, tq, tk):
    band = tq // tk
    streams = [[], []]
    for qt in range(seq // tq):
        q0 = qt * tq
        for n in range(band + band * qt):
            streams[0 if n < band else 1].append((q0, q0 + (band - 1 - n) * tk, q0))
    counts = [len(s) for s in streams]
    flat, bases = [], []
    for s in streams:
        s = s + [(0, 0, seq)] * SB_PIPELINE_DEPTH
        bases.append(len(flat))
        for field in range(SB_TABLE_FIELDS):
            flat.extend(e[field] for e in s)
    return counts, bases, jnp.asarray(flat, jnp.int32)


def _sb_attn_kernel(tab_ref, q_ref, k_ref, vt_ref, tri_ref, o_ref, z_ref, sp_ref, tot_ref, m_ref, a_ref,
                    acct_ref, ls_ref, *, tq, tk, counts, bases, unroll):
    seq = q_ref.shape[2]
    lanes = tot_ref.shape[2]
    band = tq // tk
    depth = SB_PIPELINE_DEPTH
    assert band in (1, 2)

    def run_phase(p, slot, base, length, masked, last_stage=4):
        other = 1 - slot

        def field(f, stage):
            return tab_ref[base + f * length + (p - stage)]

        if last_stage >= 4:
            key0 = pl.multiple_of(field(1, 4), tk)
            rows = pl.ds(pl.multiple_of(field(2, 4), tq), tq)
            acct_ref[:, rows] += lax.dot_general(vt_ref[0, 0, :, pl.ds(key0, tk)], a_ref[other], NT_DIMS,
                                                 preferred_element_type=F32)
        if last_stage >= 3:
            rows = pl.ds(pl.multiple_of(field(2, 3), tq), tq)
            ls = ls_ref[rows, :]
            ls_ref[rows, :] = ls + tot_ref[slot]
            a = jnp.exp2(m_ref[other] - jnp.concatenate([ls] * (tk // lanes), axis=1))
            a_ref[slot] = a.astype(BF16)
        if last_stage >= 2:
            m_ref[slot] = z_ref[slot] - _dot(sp_ref[other], tri_ref[...])
        if last_stage >= 1:
            z = z_ref[other]
            sp = jnp.maximum(jnp.log2(1.0 + jnp.exp2(jnp.minimum(z, SOFTPLUS2_DIRECT_MAX))), z)
            if masked:
                diag_shift = (band - 1 - (slot - 1) % band) * tk
                valid = (lax.broadcasted_iota(jnp.int32, (tq, tk), 1) + diag_shift
                         < lax.broadcasted_iota(jnp.int32, (tq, tk), 0))
                sp = jnp.where(valid, sp, 0.0)
                z_ref[other] = jnp.where(valid, z, MASKED_LOG2)
            sp_ref[slot] = sp.astype(BF16)
            tot_ref[slot] = jnp.broadcast_to(jnp.sum(sp, axis=-1, keepdims=True), (tq, lanes))
        qrow = pl.multiple_of(field(0, 0), tq)
        key0 = pl.multiple_of(field(1, 0), tk)
        z_ref[slot] = lax.dot_general(q_ref[0, 0, pl.ds(qrow, tq), :], k_ref[0, 0, pl.ds(key0, tk), :],
                                      NT_DIMS, preferred_element_type=F32)

    def run_stream(index, masked, unroll):
        count, base = counts[index], bases[index]
        length = count + depth
        unroll = max(u for u in range(2, unroll + 1, 2) if count % u == 0)
        assert depth % 2 == 0
        for p in range(depth):
            run_phase(p, p % 2, base, length, masked, last_stage=p)

        def steady(i, c):
            for j in range(unroll):
                run_phase(depth + unroll * i + j, j % 2, base, length, masked)
            return c

        lax.fori_loop(0, count // unroll, steady, 0)

    acct_ref[...] = jnp.zeros_like(acct_ref)
    ls_ref[...] = jnp.zeros_like(ls_ref)
    run_stream(0, True, min(unroll, SB_DIAGONAL_UNROLL))
    run_stream(1, False, unroll)

    o_ref[0, 0] = acct_ref[:, :seq].astype(o_ref.dtype)


def _sb_attn(q, k, vt, *, tq=256, tk=256, lanes=V7X_LANES, unroll=62):
    B, H, S, dh = q.shape
    idx = jnp.arange(tk)
    tri = (idx[:, None] >= idx[None, :]).astype(BF16)
    counts, bases, table = _sb_streams(S, tq, tk)
    per_head = pl.BlockSpec((1, 1, S, dh), lambda b, h, tab: (b, h, 0, 0))
    return pl.pallas_call(
        functools.partial(_sb_attn_kernel, tq=tq, tk=tk, counts=counts, bases=bases, unroll=unroll),
        grid_spec=pltpu.PrefetchScalarGridSpec(
            num_scalar_prefetch=1,
            grid=(B, H),
            in_specs=[per_head, per_head, pl.BlockSpec((1, 1, dh, S), lambda b, h, tab: (b, h, 0, 0)),
                      pl.BlockSpec(tri.shape, lambda b, h, tab: (0, 0))],
            out_specs=pl.BlockSpec((1, 1, dh, S), lambda b, h, tab: (b, h, 0, 0)),
            scratch_shapes=[
                pltpu.VMEM((2, tq, tk), F32),
                pltpu.VMEM((2, tq, tk), BF16),
                pltpu.VMEM((2, tq, lanes), F32),
                pltpu.VMEM((2, tq, tk), F32),
                pltpu.VMEM((2, tq, tk), BF16),
                pltpu.VMEM((dh, S + tq), F32),
                pltpu.VMEM((S + tq, lanes), F32),
            ],
        ),
        out_shape=jax.ShapeDtypeStruct((B, H, dh, S), BF16),
        compiler_params=_params("parallel", "parallel"),
        name="sb_attn",
    )(table, q, k, vt, tri)


def _rotary_tables(seq, dk):
    half = dk // 2
    inv_freq = ROPE_BASE ** (-jnp.arange(half, dtype=F32) / half)
    ang = jnp.arange(seq, dtype=F32)[:, None] * inv_freq[None, :]
    return jnp.cos(ang), jnp.sin(ang)


def kernel(x, mix_norm_g, w_ret_in, ret_gn_g, w_ret_out, kv_norm_g, w_kv, w_sb_q, w_sb_out,
           ffn_norm_g, w_ffn_in, w_ffn_out, final_norm_g):
    B, S, D = x.shape
    depth = mix_norm_g.shape[0]
    n_ret = w_ret_in.shape[0]
    assert w_sb_q.shape[0] == 1 and depth == n_ret + 1
    x2 = x.reshape(B * S, D)
    cos, sin = _rotary_tables(S, D // RET_HEADS)
    final_g = final_norm_g.reshape(1, D)

    for layer in range(depth):
        mix_g = mix_norm_g[layer].reshape(1, D)
        ffn_args = (ffn_norm_g[layer].reshape(1, D), w_ffn_in[layer].astype(BF16),
                    w_ffn_out[layer].astype(BF16), final_g)
        final_norm = layer == depth - 1
        if layer < n_ret:
            q, k, kdt, v, gate = _ret_proj(x2, mix_g, w_ret_in[layer].astype(BF16), cos, sin, seq=S)
            o = _retention(q, k, kdt, v, batch=B, seq=S)
            x2 = _ret_out_ffn(x2, o, gate, ret_gn_g[layer].reshape(1, -1), w_ret_out[layer].astype(BF16),
                              *ffn_args, final_norm=final_norm)
        else:
            q, k_sb, vt_sb = _sb_proj(x2, kv_norm_g.reshape(1, D), mix_g,
                                     w_kv.astype(BF16), w_sb_q[0].astype(BF16), batch=B, seq=S)
            o = _sb_attn(q, k_sb, vt_sb)
            x2 = _sb_out_ffn(x2, o, w_sb_out[0].astype(BF16), *ffn_args, seq=S, final_norm=final_norm)
    return x2.reshape(B, S, D)
```

```python
import functools

import jax
import jax.numpy as jnp
from jax import lax
from jax.experimental import pallas as pl
from jax.experimental.pallas import tpu as pltpu

RMS_EPS = 1e-6
GN_EPS = 1e-5
RET_HEADS = 4
RET_CHUNK = 256
ROPE_BASE = 10000.0
SB_HEADS = 16

LOG2_E = 1.4426950408889634

F32 = jnp.float32
BF16 = jnp.bfloat16

V7X_VMEM_BYTES = 64 * 1024 * 1024
VMEM_LIMIT_BYTES = (V7X_VMEM_BYTES * 7) // 8
V7X_LANES = 128

NT_DIMS = (((1,), (1,)), ((), ()))
TN_DIMS = (((0,), (0,)), ((), ()))


def _params(*semantics):
    return pltpu.CompilerParams(dimension_semantics=semantics, vmem_limit_bytes=VMEM_LIMIT_BYTES)


def _rms_scale(x):
    return x * lax.rsqrt(jnp.mean(x * x, axis=-1, keepdims=True) + RMS_EPS)


def _silu(g):
    return g * (1.0 / (1.0 + jnp.exp(-g)))


def _dot(a, b):
    return jnp.dot(a, b, preferred_element_type=F32)


def _ret_proj_kernel(x_ref, g_ref, w_ref, cos_ref, sin_ref, kdec_ref, q_ref, k_ref, kdt_ref, v_ref, gate_ref,
                     *, dk, k_scale):
    hb = (_rms_scale(x_ref[...]) * g_ref[...]).astype(BF16)
    cos = cos_ref[...]
    sin = sin_ref[...]
    half = dk // 2
    qk_width = q_ref.shape[1]
    v_width = v_ref.shape[1]
    for base, out_ref, scale in ((0, q_ref, 1.0), (qk_width, k_ref, k_scale)):
        for c in range(0, qk_width, dk):
            t = _dot(hb, w_ref[:, base + c:base + c + dk])
            t1, t2 = t[:, :half], t[:, half:]
            r1 = (t1 * cos - t2 * sin) * scale
            r2 = (t1 * sin + t2 * cos) * scale
            out_ref[:, c:c + half] = r1.astype(BF16)
            out_ref[:, c + half:c + dk] = r2.astype(BF16)
            if out_ref is k_ref:
                kdec = kdec_ref[c // dk]
                kdt_ref[c:c + half, :] = (r1 * kdec).T.astype(BF16)
                kdt_ref[c + half:c + dk, :] = (r2 * kdec).T.astype(BF16)
    for base, out_ref in ((2 * qk_width, v_ref), (2 * qk_width + v_width, gate_ref)):
        for c in range(0, v_width, 2 * dk):
            out_ref[:, c:c + 2 * dk] = _dot(hb, w_ref[:, base + c:base + c + 2 * dk]).astype(BF16)


def _ret_proj(x2, g, w, cos, sin, *, seq, tm=512):
    T, D = x2.shape
    dk = D // RET_HEADS
    qk_width, v_width = D, 2 * D
    sblocks = seq // tm
    row = lambda i: (i, 0)
    kdec = jnp.tile(_retention_decays()[2], (1, tm // RET_CHUNK, 1))
    return pl.pallas_call(
        functools.partial(_ret_proj_kernel, dk=dk, k_scale=dk ** -0.5),
        grid=(T // tm,),
        in_specs=[
            pl.BlockSpec((tm, D), row),
            _resident((1, D)),
            _resident(w.shape),
            pl.BlockSpec((tm, dk // 2), lambda i: (i % sblocks, 0)),
            pl.BlockSpec((tm, dk // 2), lambda i: (i % sblocks, 0)),
            _resident(kdec.shape),
        ],
        out_specs=[
            pl.BlockSpec((tm, qk_width), row),
            pl.BlockSpec((tm, qk_width), row),
            pl.BlockSpec((qk_width, tm), lambda i: (0, i)),
            pl.BlockSpec((tm, v_width), row),
            pl.BlockSpec((tm, v_width), row),
        ],
        out_shape=[
            jax.ShapeDtypeStruct((T, qk_width), BF16),
            jax.ShapeDtypeStruct((T, qk_width), BF16),
            jax.ShapeDtypeStruct((qk_width, T), BF16),
            jax.ShapeDtypeStruct((T, v_width), BF16),
            jax.ShapeDtypeStruct((T, v_width), BF16),
        ],
        compiler_params=_params("parallel"),
        name="ret_proj",
    )(x2, g, w, cos, sin, kdec)


def _retention_kernel(q_ref, k_ref, kdt_ref, v_ref, dmat_ref, qdec_ref, cdec_ref, o_ref, state_ref, *, chunk):
    @pl.when(pl.program_id(2) == 0)
    def _():
        state_ref[...] = jnp.zeros_like(state_ref)

    dmat = dmat_ref[0]
    qdec = qdec_ref[0]
    cdec = cdec_ref[0]
    n_chunks = q_ref.shape[0] // chunk
    chunks = [slice(c * chunk, (c + 1) * chunk) for c in range(n_chunks)]
    updates = [_dot(kdt_ref[:, rows], v_ref[rows, :]) for rows in chunks]
    states = [state_ref[...]]
    for upd in updates:
        states.append(states[-1] * cdec + upd)
    state_ref[...] = states[-1]
    for rows, state in zip(chunks, states):
        qc = q_ref[rows, :]
        vc = v_ref[rows, :]
        scores = lax.dot_general(qc, k_ref[rows, :], NT_DIMS, preferred_element_type=F32) * dmat
        o_ref[rows, :] = _dot(scores.astype(BF16), vc) + _dot(qc, state.astype(BF16)) * qdec


def _retention_decays():
    H, C = RET_HEADS, RET_CHUNK
    log_g = jnp.log(1.0 - 2.0 ** (-5.0 - jnp.arange(H, dtype=F32)))
    j = jnp.arange(C, dtype=F32)
    rel = j[:, None] - j[None, :]
    dmat = jnp.where(rel[None] >= 0, jnp.exp(rel[None] * log_g[:, None, None]), 0.0)
    qdec = jnp.exp((j[None, :] + 1.0) * log_g[:, None])[:, :, None]
    kdec = jnp.exp((C - 1.0 - j[None, :]) * log_g[:, None])[:, :, None]
    cdec = jnp.exp(C * log_g)[:, None, None]
    return dmat, qdec, kdec, cdec


def _retention(q, k, kdt, v, *, batch, seq, rows_per_step=2048):
    T = q.shape[0]
    rows_per_step = min(rows_per_step, seq)
    H, C = RET_HEADS, RET_CHUNK
    dk = q.shape[1] // H
    dv = v.shape[1] // H
    nsb = seq // rows_per_step
    dmat, qdec, _, cdec = _retention_decays()

    tok = lambda b, h, s: (b * nsb + s, h)
    per_head = lambda b, h, s: (h, 0, 0)
    return pl.pallas_call(
        functools.partial(_retention_kernel, chunk=C),
        grid=(batch, H, nsb),
        in_specs=[
            pl.BlockSpec((rows_per_step, dk), tok),
            pl.BlockSpec((rows_per_step, dk), tok),
            pl.BlockSpec((dk, rows_per_step), lambda b, h, s: (h, b * nsb + s)),
            pl.BlockSpec((rows_per_step, dv), tok),
            pl.BlockSpec((1, C, C), per_head),
            pl.BlockSpec((1, C, 1), per_head),
            pl.BlockSpec((1, 1, 1), per_head),
        ],
        out_specs=pl.BlockSpec((rows_per_step, dv), tok),
        out_shape=jax.ShapeDtypeStruct((T, v.shape[1]), F32),
        scratch_shapes=[pltpu.VMEM((dk, dv), F32)],
        compiler_params=_params("parallel", "parallel", "arbitrary"),
        name="retention",
    )(q, k, kdt, v, dmat, qdec, cdec)


def _ffn_tail(x1, g_ref, win_ref, wout_ref, fg_ref, out_ref, *, chunk, final_norm):
    d_ff = wout_ref.shape[0]
    hb = (_rms_scale(x1) * g_ref[...]).astype(BF16)
    out_ref[...] = x1
    for c in range(0, d_ff, chunk):
        gate = _dot(hb, win_ref[:, c:c + chunk])
        up = _dot(hb, win_ref[:, d_ff + c:d_ff + c + chunk])
        act = (_silu(gate) * up).astype(BF16)
        out_ref[...] += _dot(act, wout_ref[c:c + chunk, :])
    if final_norm:
        out_ref[...] = _rms_scale(out_ref[...]) * fg_ref[...]


def _ret_out_ffn_kernel(x_ref, o_ref, gate_ref, gn_ref, wo_ref, g_ref, win_ref, wout_ref, fg_ref, out_ref,
                        *, dv, chunk, final_norm):
    x1 = x_ref[...]
    for c in range(0, o_ref.shape[1], dv):
        o = o_ref[:, c:c + dv]
        oc = o - jnp.mean(o, axis=-1, keepdims=True)
        on = oc * lax.rsqrt(jnp.mean(oc * oc, axis=-1, keepdims=True) + GN_EPS) * gn_ref[:, c:c + dv]
        y = (_silu(gate_ref[:, c:c + dv].astype(F32)) * on).astype(BF16)
        x1 = x1 + _dot(y, wo_ref[c:c + dv, :])
    _ffn_tail(x1, g_ref, win_ref, wout_ref, fg_ref, out_ref, chunk=chunk, final_norm=final_norm)


def _sb_out_ffn_kernel(x_ref, ot_ref, wo_ref, g_ref, win_ref, wout_ref, fg_ref, out_ref, *, chunk, final_norm):
    H, dh, tm = ot_ref.shape[1:]
    heads_t = ot_ref[0].reshape(H * dh, tm)
    x1 = x_ref[...] + lax.dot_general(heads_t, wo_ref[...], TN_DIMS, preferred_element_type=F32)
    _ffn_tail(x1, g_ref, win_ref, wout_ref, fg_ref, out_ref, chunk=chunk, final_norm=final_norm)


def _resident(shape):
    return pl.BlockSpec(shape, lambda i: (0,) * len(shape), pipeline_mode=pl.Buffered(1))


def _mixer_out_ffn(body, x2, mixer_specs, mixer_args, g, w_in, w_out, final_g, *, name, tm):
    T, D = x2.shape
    row = lambda i: (i, 0)
    return pl.pallas_call(
        body,
        grid=(T // tm,),
        in_specs=[pl.BlockSpec((tm, D), row)] + mixer_specs + [
            _resident((1, D)), _resident(w_in.shape), _resident(w_out.shape), _resident((1, D))],
        out_specs=pl.BlockSpec((tm, D), row),
        out_shape=jax.ShapeDtypeStruct((T, D), F32),
        compiler_params=_params("parallel"),
        name=name,
    )(x2, *mixer_args, g, w_in, w_out, final_g)


def _ret_out_ffn(x2, o, gate, gn_g, wo, g, w_in, w_out, final_g, *, final_norm, tm=512, chunk=256):
    width = o.shape[1]
    row = lambda i: (i, 0)
    body = functools.partial(_ret_out_ffn_kernel, dv=width // RET_HEADS, chunk=chunk, final_norm=final_norm)
    specs = [pl.BlockSpec((tm, width), row), pl.BlockSpec((tm, width), row), _resident((1, width)),
             _resident(wo.shape)]
    return _mixer_out_ffn(body, x2, specs, (o, gate, gn_g, wo), g, w_in, w_out, final_g,
                          name="ret_out_ffn", tm=tm)


def _sb_out_ffn(x2, o, wo, g, w_in, w_out, final_g, *, seq, final_norm, tm=512, chunk=256):
    _, H, dh, _ = o.shape
    sblocks = seq // tm
    body = functools.partial(_sb_out_ffn_kernel, chunk=chunk, final_norm=final_norm)
    specs = [pl.BlockSpec((1, H, dh, tm), lambda i: (i // sblocks, 0, 0, i % sblocks)), _resident(wo.shape)]
    return _mixer_out_ffn(body, x2, specs, (o, wo), g, w_in, w_out, final_g, name="sb_out_ffn", tm=tm)


def _sb_proj_kernel(x_ref, gkv_ref, gq_ref, wkv_ref, wq_ref, q_ref, k_ref, vt_ref, *, dh, q_scale, group):
    r = _rms_scale(x_ref[...])
    hkv = (r * gkv_ref[...]).astype(BF16)
    hq = (r * gq_ref[...]).astype(BF16)
    D = wq_ref.shape[1]
    width = group * dh
    for c in range(0, D, width):
        tq = _dot(hq, wq_ref[:, c:c + width]) * q_scale
        tk = _dot(hkv, wkv_ref[:, c:c + width])
        tv = _dot(hkv, wkv_ref[:, D + c:D + c + width])
        for g in range(group):
            h = c // dh + g
            lanes = slice(g * dh, (g + 1) * dh)
            q_ref[0, h] = tq[:, lanes].astype(BF16)
            k_ref[0, h] = tk[:, lanes].astype(BF16)
            vt_ref[0, h] = tv[:, lanes].T.astype(BF16)


def _sb_proj(x2, gkv, gq, wkv, wq, *, batch, seq, tm=512):
    T, D = x2.shape
    H = SB_HEADS
    dh = D // H
    sblocks = seq // tm
    const = lambda i: (0, 0)
    head_major = pl.BlockSpec((1, H, tm, dh), lambda i: (i // sblocks, 0, i % sblocks, 0))
    shape = jax.ShapeDtypeStruct((batch, H, seq, dh), BF16)
    return pl.pallas_call(
        functools.partial(_sb_proj_kernel, dh=dh, q_scale=dh ** -0.5 * LOG2_E, group=4),
        grid=(T // tm,),
        in_specs=[
            pl.BlockSpec((tm, D), lambda i: (i, 0)),
            pl.BlockSpec((1, D), const),
            pl.BlockSpec((1, D), const),
            pl.BlockSpec(wkv.shape, const),
            pl.BlockSpec(wq.shape, const),
        ],
        out_specs=[head_major, head_major,
                   pl.BlockSpec((1, H, dh, tm), lambda i: (i // sblocks, 0, 0, i % sblocks))],
        out_shape=[shape, shape, jax.ShapeDtypeStruct((batch, H, dh, seq), BF16)],
        compiler_params=_params("parallel"),
        name="sb_proj",
    )(x2, gkv, gq, wkv, wq)


MASKED_LOG2 = -1e30
SOFTPLUS2_DIRECT_MAX = 126.0
SB_PIPELINE_DEPTH = 4
SB_TABLE_FIELDS = 3
SB_DIAGONAL_UNROLL = 16
SB_ROW_SPLIT = 2


def _sb_streams(seq, tq, tk):
    band = tq // tk
    streams = [[], []]
    for qt in range(seq // tq):
        q0 = qt * tq
        for n in range(band + band * qt):
            streams[0 if n < band else 1].append((q0, q0 + (band - 1 - n) * tk, q0))
    counts = [len(s) for s in streams]
    flat, bases = [], []
    for s in streams:
        s = s + [(0, 0, seq)] * SB_PIPELINE_DEPTH
        bases.append(len(flat))
        for field in range(SB_TABLE_FIELDS):
            flat.extend(e[field] for e in s)
    return counts, bases, jnp.asarray(flat, jnp.int32)


def _sb_attn_kernel(tab_ref, q_ref, k_ref, vt_ref, tri_ref, o_ref, z_ref, sp_ref, tot_ref, m_ref, a_ref,
                    acct_ref, ls_ref, *, tq, tk, counts, bases, unroll):
    seq = q_ref.shape[2]
    lanes = tot_ref.shape[2]
    band = tq // tk
    depth = SB_PIPELINE_DEPTH
    assert band in (1, 2)

    def run_phase(p, slot, base, length, masked, last_stage=4):
        other = 1 - slot

        def field(f, stage):
            return tab_ref[base + f * length + (p - stage)]

        if last_stage >= 4:
            key0 = pl.multiple_of(field(1, 4), tk)
            rows = pl.ds(pl.multiple_of(field(2, 4), tq), tq)
            acct_ref[:, rows] += lax.dot_general(vt_ref[0, 0, :, pl.ds(key0, tk)], a_ref[other], NT_DIMS,
                                                 preferred_element_type=F32)
        if last_stage >= 3:
            rows = pl.ds(pl.multiple_of(field(2, 3), tq), tq)
            ls = ls_ref[rows, :]
            ls_ref[rows, :] = ls + tot_ref[slot]
            a = jnp.exp2(m_ref[other] - jnp.concatenate([ls] * (tk // lanes), axis=1))
            a_ref[slot] = a.astype(BF16)
        if last_stage >= 2:
            for r in range(0, tq, tq // SB_ROW_SPLIT):
                piece = slice(r, r + tq // SB_ROW_SPLIT)
                m_ref[slot, piece] = z_ref[slot, piece] - _dot(sp_ref[other, piece], tri_ref[...])
        if last_stage >= 1:
            z = z_ref[other]
            sp = jnp.maximum(jnp.log2(1.0 + jnp.exp2(jnp.minimum(z, SOFTPLUS2_DIRECT_MAX))), z)
            if masked:
                diag_shift = (band - 1 - (slot - 1) % band) * tk
                valid = (lax.broadcasted_iota(jnp.int32, (tq, tk), 1) + diag_shift
                         < lax.broadcasted_iota(jnp.int32, (tq, tk), 0))
                sp = jnp.where(valid, sp, 0.0)
                z_ref[other] = jnp.where(valid, z, MASKED_LOG2)
            sp_ref[slot] = sp.astype(BF16)
            tot_ref[slot] = jnp.broadcast_to(jnp.sum(sp, axis=-1, keepdims=True), (tq, lanes))
        qrow = pl.multiple_of(field(0, 0), tq)
        key0 = pl.multiple_of(field(1, 0), tk)
        keys = k_ref[0, 0, pl.ds(key0, tk), :]
        for r in range(0, tq, tq // SB_ROW_SPLIT):
            rows = pl.ds(pl.multiple_of(qrow + r, tq // SB_ROW_SPLIT), tq // SB_ROW_SPLIT)
            z_ref[slot, r:r + tq // SB_ROW_SPLIT] = lax.dot_general(q_ref[0, 0, rows, :], keys, NT_DIMS,
                                                                    preferred_element_type=F32)

    def run_stream(index, masked, unroll):
        count, base = counts[index], bases[index]
        length = count + depth
        unroll = max(u for u in range(2, unroll + 1, 2) if count % u == 0)
        assert depth % 2 == 0
        for p in range(depth):
            run_phase(p, p % 2, base, length, masked, last_stage=p)

        def steady(i, c):
            for j in range(unroll):
                run_phase(depth + unroll * i + j, j % 2, base, length, masked)
            return c

        lax.fori_loop(0, count // unroll, steady, 0)

    acct_ref[...] = jnp.zeros_like(acct_ref)
    ls_ref[...] = jnp.zeros_like(ls_ref)
    run_stream(0, True, min(unroll, SB_DIAGONAL_UNROLL))
    run_stream(1, False, unroll)

    o_ref[0, 0] = acct_ref[:, :seq].astype(o_ref.dtype)


def _sb_attn(q, k, vt, *, tq=256, tk=256, lanes=V7X_LANES, unroll=62):
    B, H, S, dh = q.shape
    idx = jnp.arange(tk)
    tri = (idx[:, None] >= idx[None, :]).astype(BF16)
    counts, bases, table = _sb_streams(S, tq, tk)
    per_head = pl.BlockSpec((1, 1, S, dh), lambda b, h, tab: (b, h, 0, 0))
    return pl.pallas_call(
        functools.partial(_sb_attn_kernel, tq=tq, tk=tk, counts=counts, bases=bases, unroll=unroll),
        grid_spec=pltpu.PrefetchScalarGridSpec(
            num_scalar_prefetch=1,
            grid=(B, H),
            in_specs=[per_head, per_head, pl.BlockSpec((1, 1, dh, S), lambda b, h, tab: (b, h, 0, 0)),
                      pl.BlockSpec(tri.shape, lambda b, h, tab: (0, 0))],
            out_specs=pl.BlockSpec((1, 1, dh, S), lambda b, h, tab: (b, h, 0, 0)),
            scratch_shapes=[
                pltpu.VMEM((2, tq, tk), F32),
                pltpu.VMEM((2, tq, tk), BF16),
                pltpu.VMEM((2, tq, lanes), F32),
                pltpu.VMEM((2, tq, tk), F32),
                pltpu.VMEM((2, tq, tk), BF16),
                pltpu.VMEM((dh, S + tq), F32),
                pltpu.VMEM((S + tq, lanes), F32),
            ],
        ),
        out_shape=jax.ShapeDtypeStruct((B, H, dh, S), BF16),
        compiler_params=_params("parallel", "parallel"),
        name="sb_attn",
    )(table, q, k, vt, tri)


def _rotary_tables(seq, dk):
    half = dk // 2
    inv_freq = ROPE_BASE ** (-jnp.arange(half, dtype=F32) / half)
    ang = jnp.arange(seq, dtype=F32)[:, None] * inv_freq[None, :]
    return jnp.cos(ang), jnp.sin(ang)


def kernel(x, mix_norm_g, w_ret_in, ret_gn_g, w_ret_out, kv_norm_g, w_kv, w_sb_q, w_sb_out,
           ffn_norm_g, w_ffn_in, w_ffn_out, final_norm_g):
    B, S, D = x.shape
    depth = mix_norm_g.shape[0]
    n_ret = w_ret_in.shape[0]
    assert w_sb_q.shape[0] == 1 and depth == n_ret + 1
    x2 = x.reshape(B * S, D)
    cos, sin = _rotary_tables(S, D // RET_HEADS)
    final_g = final_norm_g.reshape(1, D)

    for layer in range(depth):
        mix_g = mix_norm_g[layer].reshape(1, D)
        ffn_args = (ffn_norm_g[layer].reshape(1, D), w_ffn_in[layer].astype(BF16),
                    w_ffn_out[layer].astype(BF16), final_g)
        final_norm = layer == depth - 1
        if layer < n_ret:
            q, k, kdt, v, gate = _ret_proj(x2, mix_g, w_ret_in[layer].astype(BF16), cos, sin, seq=S)
            o = _retention(q, k, kdt, v, batch=B, seq=S)
            x2 = _ret_out_ffn(x2, o, gate, ret_gn_g[layer].reshape(1, -1), w_ret_out[layer].astype(BF16),
                              *ffn_args, final_norm=final_norm)
        else:
            q, k_sb, vt_sb = _sb_proj(x2, kv_norm_g.reshape(1, D), mix_g,
                                     w_kv.astype(BF16), w_sb_q[0].astype(BF16), batch=B, seq=S)
            o = _sb_attn(q, k_sb, vt_sb)
            x2 = _sb_out_ffn(x2, o, w_sb_out[0].astype(BF16), *ffn_args, seq=S, final_norm=final_norm)
    return x2.reshape(B, S, D)
```
